```python
import math
import jax, jax.numpy as jnp
from jax import lax
import numpy as np

D_MODEL = 2048
BATCH = 2
SEQ = 4096
DEPTH = 1

CHUNK = 64
Q_BLOCK = 128
HEAD_DIM = 128
N_FOX_HEADS = 8
N_SB_HEADS = 8
D_FOX = N_FOX_HEADS * HEAD_DIM
D_SB = N_SB_HEADS * HEAD_DIM
D_MIX = D_FOX + D_SB
D_IN = 4 * D_FOX + 3 * D_SB + N_FOX_HEADS
N_KEYS = 128
N_EXPERTS = N_KEYS * N_KEYS
PEER_HEADS = 8
PEER_TOPK = 16
PEER_DKEY = 256
PEER_HALF = PEER_DKEY // 2
PEER_TOK_BLOCK = 128
EPS = 1e-6

kernel_name = "fox_stickbreak_peer_hybrid_block"


def rmsnorm(x, g):
    xf = x.astype(jnp.float32)
    r = lax.rsqrt(jnp.mean(xf * xf, axis=-1, keepdims=True) + EPS)
    return (xf * r).astype(x.dtype) * g


def fox_block(q_blk, k_ctx, v_ctx, c_q, c_k, t0):
    logits = jnp.einsum('bhqd,bhkd->bhqk', q_blk, k_ctx,
                        preferred_element_type=jnp.float32) * (HEAD_DIM ** -0.5)
    logits = logits + c_q[..., :, None] - c_k[..., None, :]
    tq = t0 + jnp.arange(q_blk.shape[2])
    tk = jnp.arange(k_ctx.shape[2])
    mask = tk[None, :] <= tq[:, None]
    logits = jnp.where(mask, logits, -jnp.inf)
    p = jax.nn.softmax(logits, axis=-1)
    return jnp.einsum('bhqk,bhkd->bhqd', p.astype(v_ctx.dtype), v_ctx)


def stick_breaking_block(q_blk, k_ctx, v_ctx, t0):
    z = jnp.einsum('bhqd,bhkd->bhqk', q_blk, k_ctx,
                   preferred_element_type=jnp.float32) * (HEAD_DIM ** -0.5)
    tq = t0 + jnp.arange(q_blk.shape[2])
    tk = jnp.arange(k_ctx.shape[2])
    mask = tk[None, :] < tq[:, None]
    log_beta = jax.nn.log_sigmoid(z)
    log_1mb = jnp.where(mask, jax.nn.log_sigmoid(-z), 0.0)
    suffix = lax.cumsum(log_1mb, axis=3, reverse=True) - log_1mb
    a = jnp.where(mask, jnp.exp(log_beta + suffix), 0.0)
    return jnp.einsum('bhqk,bhkd->bhqd', a.astype(v_ctx.dtype), v_ctx)


def hybrid_mixer(xn, w_in, b_forget, q_norm_g, k_norm_g, w_out):
    B, S, _ = xn.shape
    proj = xn @ w_in

    def heads(t, n_heads):
        return t.reshape(B, S, n_heads, HEAD_DIM).transpose(0, 2, 1, 3)

    o = 0
    q_f = rmsnorm(heads(proj[..., o:o + D_FOX], N_FOX_HEADS), q_norm_g); o += D_FOX
    k_f = rmsnorm(heads(proj[..., o:o + D_FOX], N_FOX_HEADS), k_norm_g); o += D_FOX
    v_f = heads(proj[..., o:o + D_FOX], N_FOX_HEADS); o += D_FOX
    gate_f = jax.nn.sigmoid(proj[..., o:o + D_FOX]); o += D_FOX
    q_s = heads(proj[..., o:o + D_SB], N_SB_HEADS); o += D_SB
    k_s = heads(proj[..., o:o + D_SB], N_SB_HEADS); o += D_SB
    v_s = heads(proj[..., o:o + D_SB], N_SB_HEADS); o += D_SB
    f_logit = proj[..., o:o + N_FOX_HEADS].astype(jnp.float32) + b_forget
    log_f = jax.nn.log_sigmoid(f_logit)
    c = jnp.cumsum(log_f, axis=1).transpose(0, 2, 1)

    fox_out, sb_out = [], []
    for i in range(S // Q_BLOCK):
        t0 = i * Q_BLOCK
        end = t0 + Q_BLOCK
        fox_out.append(fox_block(q_f[:, :, t0:end], k_f[:, :, :end], v_f[:, :, :end],
                                 c[:, :, t0:end], c[:, :, :end], t0))
        sb_out.append(stick_breaking_block(q_s[:, :, t0:end], k_s[:, :, :end],
                                           v_s[:, :, :end], t0))
    fox = jnp.concatenate(fox_out, axis=2).transpose(0, 2, 1, 3).reshape(B, S, D_FOX)
    sb = jnp.concatenate(sb_out, axis=2).transpose(0, 2, 1, 3).reshape(B, S, D_SB)
    mixed = jnp.concatenate([fox * gate_f, sb], axis=-1)
    return mixed @ w_out


def peer_ffn(xn, w_pq, sub_keys, u, v):
    B, S, D = xn.shape
    T = B * S
    xt = xn.reshape(T, D)
    q = (xt @ w_pq).reshape(T, PEER_HEADS, 2, PEER_HALF)
    s = jnp.einsum('thcd,hcnd->thcn', q, sub_keys,
                   preferred_element_type=jnp.float32)
    s1, i1 = lax.top_k(s[:, :, 0], PEER_TOPK)
    s2, i2 = lax.top_k(s[:, :, 1], PEER_TOPK)
    cand = (s1[..., :, None] + s2[..., None, :]).reshape(T, PEER_HEADS, PEER_TOPK * PEER_TOPK)
    cidx = (i1[..., :, None] * N_KEYS + i2[..., None, :]).reshape(T, PEER_HEADS, PEER_TOPK * PEER_TOPK)
    top_s, pos = lax.top_k(cand, PEER_TOPK)
    idx = jnp.take_along_axis(cidx, pos, axis=-1)
    g = jax.nn.softmax(top_s, axis=-1).astype(xn.dtype)
    n_blk = T // PEER_TOK_BLOCK
    E = PEER_HEADS * PEER_TOPK

    def block(args):
        xb, ib, gb = args
        ub = u[ib]
        h = jax.nn.gelu(jnp.einsum('ted,td->te', ub, xb), approximate=False)
        vb = v[ib]
        return jnp.einsum('te,ted->td', gb * h, vb)

    out = lax.map(block, (xt.reshape(n_blk, PEER_TOK_BLOCK, D),
                          idx.reshape(n_blk, PEER_TOK_BLOCK, E),
                          g.reshape(n_blk, PEER_TOK_BLOCK, E)))
    return out.reshape(B, S, D)


def setup_inputs(seed: int = 0) -> dict:
    key = jax.random.key(seed)
    ks = jax.random.split(key, 14)
    f32 = jnp.float32
    nrm = lambda k, shape, scale: jax.random.normal(k, shape, f32) * scale
    return {
        "x": jax.random.normal(ks[0], (BATCH, SEQ, D_MODEL), f32),
        "attn_norm_g": 1.0 + nrm(ks[1], (DEPTH, D_MODEL), 0.02),
        "w_in": nrm(ks[2], (DEPTH, D_MODEL, D_IN), D_MODEL ** -0.5),
        "b_forget": 3.0 + nrm(ks[3], (DEPTH, N_FOX_HEADS), 0.5),
        "q_norm_g": 1.0 + nrm(ks[4], (DEPTH, HEAD_DIM), 0.02),
        "k_norm_g": 1.0 + nrm(ks[5], (DEPTH, HEAD_DIM), 0.02),
        "w_out": nrm(ks[6], (DEPTH, D_MIX, D_MODEL), D_MIX ** -0.5),
        "ffn_norm_g": 1.0 + nrm(ks[7], (DEPTH, D_MODEL), 0.02),
        "w_peer_q": nrm(ks[8], (DEPTH, D_MODEL, PEER_HEADS * PEER_DKEY), D_MODEL ** -0.5),
        "peer_sub_keys": nrm(ks[9], (DEPTH, PEER_HEADS, 2, N_KEYS, PEER_HALF), PEER_HALF ** -0.5),
        "peer_u": nrm(ks[10], (DEPTH, N_EXPERTS, D_MODEL), D_MODEL ** -0.5),
        "peer_v": nrm(ks[11], (DEPTH, N_EXPERTS, D_MODEL), PEER_HEADS ** -0.5),
    }


def reference(x, attn_norm_g, w_in, b_forget, q_norm_g, k_norm_g, w_out,
              ffn_norm_g, w_peer_q, peer_sub_keys, peer_u, peer_v):
    h = x
    for l in range(DEPTH):
        xn = rmsnorm(h, attn_norm_g[l])
        h = h + hybrid_mixer(xn, w_in[l], b_forget[l], q_norm_g[l], k_norm_g[l], w_out[l])
        hn = rmsnorm(h, ffn_norm_g[l])
        h = h + peer_ffn(hn, w_peer_q[l], peer_sub_keys[l], peer_u[l], peer_v[l])
    return h
```

```python
import functools

import jax
import jax.numpy as jnp
from jax import lax
from jax.experimental import pallas as pl
from jax.experimental.pallas import tpu as pltpu

F32 = jnp.float32
BF16 = jnp.bfloat16
EPS = 1e-6
HEAD_DIM = 128
N_HEADS = 8
D_GROUP = N_HEADS * HEAD_DIM
N_KEYS = 128
PEER_HEADS = 8
TOPK = 16
LANES = 128
SUBLANES = 8
NEG_INF = float("-inf")
NOT_SELECTED = 99.0
VMEM_LIMIT = 56 * 1024 * 1024


def _cparams(sem):
    return pltpu.CompilerParams(dimension_semantics=sem, vmem_limit_bytes=VMEM_LIMIT)


def _log_sigmoid(z):
    return jnp.minimum(z, 0.0) - jnp.log1p(jnp.exp(-jnp.abs(z)))


def _inproj_kernel(x_ref, g_ref, w_ref, wf_ref, qg_ref, kg_ref,
                   out_ref, gate_ref, flog_ref, xn_scr):
    j = pl.program_id(1)

    @pl.when(j == 0)
    def _():
        x = x_ref[...]
        r = lax.rsqrt(jnp.mean(x * x, axis=-1, keepdims=True) + EPS)
        xn = ((x * r) * g_ref[...]).astype(BF16)
        xn_scr[...] = xn
        flog_ref[...] = lax.dot_general(wf_ref[...], xn, (((1,), (1,)), ((), ())),
                                        preferred_element_type=F32)

    p = jnp.dot(xn_scr[...], w_ref[...], preferred_element_type=F32)

    def head_norm(g):
        for h in range(N_HEADS):
            ph = p[:, h * HEAD_DIM:(h + 1) * HEAD_DIM]
            r = lax.rsqrt(jnp.mean(ph * ph, axis=-1, keepdims=True) + EPS)
            out_ref[0, :, h * HEAD_DIM:(h + 1) * HEAD_DIM] = ((ph * r) * g).astype(BF16)

    @pl.when(j == 0)
    def _():
        head_norm(qg_ref[...])

    @pl.when(j == 1)
    def _():
        head_norm(kg_ref[...])

    @pl.when(j == 3)
    def _():
        gate_ref[...] = jax.nn.sigmoid(p)
        out_ref[0] = p.astype(BF16)

    @pl.when((j == 2) | (j >= 4))
    def _():
        out_ref[0] = p.astype(BF16)


def _inproj(x2, g, w_main, w_f, qg, kg, tm=512):
    T, D = x2.shape
    n_groups = w_main.shape[1] // D_GROUP
    return pl.pallas_call(
        _inproj_kernel,
        grid=(T // tm, n_groups),
        in_specs=[
            pl.BlockSpec((tm, D), lambda i, j: (i, 0)),
            pl.BlockSpec((1, D), lambda i, j: (0, 0)),
            pl.BlockSpec((D, D_GROUP), lambda i, j: (0, j)),
            pl.BlockSpec((N_HEADS, D), lambda i, j: (0, 0)),
            pl.BlockSpec((1, HEAD_DIM), lambda i, j: (0, 0)),
            pl.BlockSpec((1, HEAD_DIM), lambda i, j: (0, 0)),
        ],
        out_specs=[
            pl.BlockSpec((1, tm, D_GROUP), lambda i, j: (j, i, 0)),
            pl.BlockSpec((tm, D_GROUP), lambda i, j: (i, 0)),
            pl.BlockSpec((N_HEADS, tm), lambda i, j: (0, i)),
        ],
        out_shape=[
            jax.ShapeDtypeStruct((n_groups, T, D_GROUP), BF16),
            jax.ShapeDtypeStruct((T, D_GROUP), F32),
            jax.ShapeDtypeStruct((N_HEADS, T), F32),
        ],
        scratch_shapes=[pltpu.VMEM((tm, D), BF16)],
        compiler_params=_cparams(("arbitrary", "arbitrary")),
        name="inproj",
    )(x2, g, w_main, w_f, qg, kg)


def _cumsum_kernel(flog_ref, b_ref, c_ref, *, seq):
    n_batch = flog_ref.shape[1] // seq
    lane = lax.broadcasted_iota(jnp.int32, (N_HEADS, seq), 1)
    for b in range(n_batch):
        x = _log_sigmoid(flog_ref[:, b * seq:(b + 1) * seq] + b_ref[...])
        sh = 1
        while sh < seq:
            x = x + jnp.where(lane >= sh, pltpu.roll(x, sh, axis=1), 0.0)
            sh *= 2
        c_ref[b * N_HEADS:(b + 1) * N_HEADS, :] = x


def _forget_cumsum(flog_t, b_forget, seq):
    T = flog_t.shape[1]
    n_batch = T // seq
    return pl.pallas_call(
        functools.partial(_cumsum_kernel, seq=seq),
        out_shape=jax.ShapeDtypeStruct((n_batch * N_HEADS, seq), F32),
        compiler_params=pltpu.CompilerParams(vmem_limit_bytes=VMEM_LIMIT),
        name="forget_cumsum",
    )(flog_t, b_forget.reshape(N_HEADS, 1))


def _fox_kernel(q_ref, k_ref, v_ref, gate_ref, cq_ref, ck_ref, o_ref,
                m_scr, l_scr, acc_scr, *, tq, scale):
    qi = pl.program_id(1)
    q = q_ref[0]
    cq = cq_ref[0]
    m_scr[...] = jnp.full(m_scr.shape, NEG_INF, F32)
    l_scr[...] = jnp.zeros(l_scr.shape, F32)
    acc_scr[...] = jnp.zeros(acc_scr.shape, F32)

    def step(kb, masked):
        off = pl.multiple_of(kb * tq, tq)
        k = k_ref[0, pl.ds(off, tq), :]
        v = v_ref[0, pl.ds(off, tq), :]
        ck = ck_ref[0, :, pl.ds(off, tq)]
        s = lax.dot_general(q, k, (((1,), (1,)), ((), ())), preferred_element_type=F32)
        s = s * scale + cq - ck
        if masked:
            row = lax.broadcasted_iota(jnp.int32, (tq, tq), 0)
            col = lax.broadcasted_iota(jnp.int32, (tq, tq), 1)
            s = jnp.where(col <= row, s, NEG_INF)
        m_prev = m_scr[...]
        m_new = jnp.maximum(m_prev, jnp.max(s, axis=1, keepdims=True))
        alpha = jnp.exp(m_prev - m_new)
        p = jnp.exp(s - m_new)
        l_scr[...] = alpha * l_scr[...] + jnp.sum(p, axis=1, keepdims=True)
        acc_scr[...] = alpha * acc_scr[...] + jnp.dot(p.astype(BF16), v,
                                                      preferred_element_type=F32)
        m_scr[...] = m_new

    def body(kb, carry):
        step(kb, False)
        return carry

    lax.fori_loop(0, qi, body, 0)
    step(qi, True)
    o_ref[...] = ((acc_scr[...] / l_scr[...]) * gate_ref[...]).astype(BF16)


def _fox_attention(proj, gate, c_col, c_row, n_batch, seq, tq=512):
    T = proj.shape[1]
    nq = seq // tq
    return pl.pallas_call(
        functools.partial(_fox_kernel, tq=tq, scale=HEAD_DIM ** -0.5),
        grid=(n_batch * N_HEADS, nq),
        in_specs=[
            pl.BlockSpec((1, tq, HEAD_DIM), lambda bh, i: (0, (bh // N_HEADS) * nq + i, bh % N_HEADS)),
            pl.BlockSpec((1, seq, HEAD_DIM), lambda bh, i: (1, bh // N_HEADS, bh % N_HEADS)),
            pl.BlockSpec((1, seq, HEAD_DIM), lambda bh, i: (2, bh // N_HEADS, bh % N_HEADS)),
            pl.BlockSpec((tq, HEAD_DIM), lambda bh, i: ((bh // N_HEADS) * nq + i, bh % N_HEADS)),
            pl.BlockSpec((1, tq, 1), lambda bh, i: (bh, i, 0)),
            pl.BlockSpec((1, 1, seq), lambda bh, i: (bh, 0, 0)),
        ],
        out_specs=pl.BlockSpec((tq, HEAD_DIM), lambda bh, i: ((bh // N_HEADS) * nq + i, bh % N_HEADS)),
        out_shape=jax.ShapeDtypeStruct((T, D_GROUP), BF16),
        scratch_shapes=[pltpu.VMEM((tq, 1), F32), pltpu.VMEM((tq, 1), F32),
                        pltpu.VMEM((tq, HEAD_DIM), F32)],
        compiler_params=_cparams(("arbitrary", "arbitrary")),
        name="fox_attention",
    )(proj, proj, proj, gate, c_col, c_row)


def _sb_kernel(q_ref, k_ref, v_ref, uo_ref, o_ref, acc_scr, carry_scr, *, tq, scale):
    qi = pl.program_id(1)
    q = q_ref[0]
    n_sub = tq // LANES
    acc_scr[...] = jnp.zeros(acc_scr.shape, F32)
    carry_scr[...] = jnp.zeros(carry_scr.shape, F32)

    def step(kc, rel):
        off = pl.multiple_of(kc * LANES, LANES)
        k = k_ref[0, pl.ds(off, LANES), :]
        v = v_ref[0, pl.ds(off, LANES), :]
        z = lax.dot_general(q, k, (((1,), (1,)), ((), ())), preferred_element_type=F32) * scale
        lp = jnp.log1p(jnp.exp(-jnp.abs(z)))
        t = jnp.minimum(z, 0.0)
        log_beta = t - lp
        log_1mb = (t - z) - lp
        if rel is not None:
            row = lax.broadcasted_iota(jnp.int32, (tq, LANES), 0)
            col = lax.broadcasted_iota(jnp.int32, (tq, LANES), 1)
            mask = (col + rel) < row
            log_1mb = jnp.where(mask, log_1mb, 0.0)
        hi = log_1mb.astype(BF16)
        lo = (log_1mb - hi.astype(F32)).astype(BF16)
        cs = jnp.dot(jnp.concatenate([hi, lo], axis=1), uo_ref[...],
                     preferred_element_type=F32)
        carry = carry_scr[...]
        a = jnp.exp(log_beta + cs[:, :LANES] + carry)
        if rel is not None:
            a = jnp.where(mask, a, 0.0)
        acc_scr[...] += jnp.dot(a.astype(BF16), v, preferred_element_type=F32)
        carry_scr[...] = carry + cs[:, LANES:]

    for d in reversed(range(n_sub)):
        step(qi * n_sub + d, d * LANES)
    n_rest = qi * n_sub

    def body(it, c):
        step(n_rest - 1 - it, None)
        return c

    lax.fori_loop(0, n_rest, body, 0)
    o_ref[...] = acc_scr[...].astype(BF16)


def _suffix_matrix():
    j = jnp.arange(2 * LANES)[:, None] % LANES
    s = jnp.arange(LANES)[None, :]
    tri = (j > s).astype(BF16)
    return jnp.concatenate([tri, jnp.ones((2 * LANES, LANES), BF16)], axis=1)


def _sb_attention(proj, n_batch, seq, tq=512):
    T = proj.shape[1]
    nq = seq // tq
    return pl.pallas_call(
        functools.partial(_sb_kernel, tq=tq, scale=HEAD_DIM ** -0.5),
        grid=(n_batch * N_HEADS, nq),
        in_specs=[
            pl.BlockSpec((1, tq, HEAD_DIM), lambda bh, i: (4, (bh // N_HEADS) * nq + i, bh % N_HEADS)),
            pl.BlockSpec((1, seq, HEAD_DIM), lambda bh, i: (5, bh // N_HEADS, bh % N_HEADS)),
            pl.BlockSpec((1, seq, HEAD_DIM), lambda bh, i: (6, bh // N_HEADS, bh % N_HEADS)),
            pl.BlockSpec((2 * LANES, 2 * LANES), lambda bh, i: (0, 0)),
        ],
        out_specs=pl.BlockSpec((tq, HEAD_DIM), lambda bh, i: ((bh // N_HEADS) * nq + i, bh % N_HEADS)),
        out_shape=jax.ShapeDtypeStruct((T, D_GROUP), BF16),
        scratch_shapes=[pltpu.VMEM((tq, HEAD_DIM), F32), pltpu.VMEM((tq, LANES), F32)],
        compiler_params=_cparams(("arbitrary", "arbitrary")),
        name="sb_attention",
    )(proj, proj, proj, _suffix_matrix())


def _outproj_kernel(fox_ref, sb_ref, x_ref, wo1_ref, wo2_ref, g_ref, h_ref, hnt_ref):
    h = (x_ref[...]
         + jnp.dot(fox_ref[...], wo1_ref[...], preferred_element_type=F32)
         + jnp.dot(sb_ref[...], wo2_ref[...], preferred_element_type=F32))
    h_ref[...] = h
    r = lax.rsqrt(jnp.mean(h * h, axis=-1, keepdims=True) + EPS)
    hn = (h * r) * g_ref[...]
    hnt_ref[...] = hn.T.astype(BF16)


def _outproj(fox, sb, x2, wo1, wo2, g, tm=512):
    T, D = x2.shape
    return pl.pallas_call(
        _outproj_kernel,
        grid=(T // tm,),
        in_specs=[
            pl.BlockSpec((tm, D_GROUP), lambda i: (i, 0)),
            pl.BlockSpec((tm, D_GROUP), lambda i: (i, 0)),
            pl.BlockSpec((tm, D), lambda i: (i, 0)),
            pl.BlockSpec((D_GROUP, D), lambda i: (0, 0)),
            pl.BlockSpec((D_GROUP, D), lambda i: (0, 0)),
            pl.BlockSpec((1, D), lambda i: (0, 0)),
        ],
        out_specs=[
            pl.BlockSpec((tm, D), lambda i: (i, 0)),
            pl.BlockSpec((D, tm), lambda i: (0, i)),
        ],
        out_shape=[jax.ShapeDtypeStruct((T, D), F32), jax.ShapeDtypeStruct((D, T), BF16)],
        compiler_params=_cparams(("arbitrary",)),
        name="outproj",
    )(fox, sb, x2, wo1, wo2, g)


def _max0(x):
    return jnp.max(x, axis=0, keepdims=True)


def _min0(x):
    return jnp.min(x, axis=0, keepdims=True)


def _top16(s, key_iota, sub_iota):
    rank = jnp.full(s.shape, NOT_SELECTED, F32)
    ew = jnp.zeros(s.shape, F32)
    v_lo = jnp.zeros((SUBLANES, LANES), F32)
    v_hi = jnp.zeros((SUBLANES, LANES), F32)
    vals = []
    for it in range(TOPK):
        m = _max0(s)
        first = _min0(jnp.where(s == m, key_iota, float(N_KEYS)))
        hit = key_iota == first
        s = jnp.where(hit, NEG_INF, s)
        rank = jnp.where(hit, float(it), rank)
        ew = jnp.where(hit, jnp.exp(m - vals[0]) if vals else 1.0, ew)
        if it < SUBLANES:
            v_lo = jnp.where(sub_iota == float(it), m, v_lo)
        else:
            v_hi = jnp.where(sub_iota == float(it - SUBLANES), m, v_hi)
        vals.append(m)
    return vals, rank, ew, v_lo, v_hi


def _select16(vals1, v2_lo, v2_hi, sub_iota):
    slabs, pos, n_invalid = [], [], []
    for a in range(TOPK):
        nb = min(TOPK // (a + 1), SUBLANES)
        c = vals1[a] + v2_lo
        if nb < SUBLANES:
            c = jnp.where(sub_iota < float(nb), c, NEG_INF)
        slabs.append(c)
        pos.append(sub_iota + float(TOPK * a))
        n_invalid.append(float(SUBLANES - nb))
    slabs.append(vals1[0] + v2_hi)
    pos.append(sub_iota + float(SUBLANES))
    n_invalid.append(0.0)

    def tree(op, xs):
        xs = list(xs)
        while len(xs) > 1:
            xs = [op(xs[i], xs[i + 1]) if i + 1 < len(xs) else xs[i]
                  for i in range(0, len(xs), 2)]
        return xs[0]

    z = None
    c_max = None
    for it in range(TOPK):
        m = _max0(tree(jnp.maximum, slabs))
        first = _min0(tree(jnp.minimum,
                           [jnp.where(c == m, p, 999.0) for c, p in zip(slabs, pos)]))
        slabs = [jnp.where(p == first, NEG_INF, c) for c, p in zip(slabs, pos)]
        if it == 0:
            c_max = m
            z = jnp.ones_like(m)
        else:
            z = z + jnp.exp(m - c_max)
    counts = []
    for i, c in enumerate(slabs):
        taken = jnp.sum(jnp.where(c == NEG_INF, 1.0, 0.0), axis=0, keepdims=True)
        counts.append(taken - n_invalid[i])
    counts[0] = counts[0] + counts.pop()
    return counts, z


def _route_kernel(hnt_ref, wq_ref, keys_ref, a1_ref, lrow_ref, r2_ref, a2_ref,
                  q_scr, s_scr, *, tb):
    qt = jnp.dot(wq_ref[...], hnt_ref[...], preferred_element_type=F32)
    q_scr[...] = qt.astype(BF16)
    for hc in range(2 * PEER_HEADS):
        rows = slice(hc * N_KEYS, (hc + 1) * N_KEYS)
        s_scr[rows, :] = jnp.dot(keys_ref[hc], q_scr[rows, :], preferred_element_type=F32)

    n_lane_groups = tb // LANES
    key_iota = lax.broadcasted_iota(jnp.int32, (N_KEYS, LANES), 0).astype(F32)
    sub_iota = lax.broadcasted_iota(jnp.int32, (SUBLANES, LANES), 0).astype(F32)

    def body(idx, carry):
        h = idx // n_lane_groups
        lanes = pl.ds(pl.multiple_of((idx % n_lane_groups) * LANES, LANES), LANES)
        s1 = s_scr[pl.ds(pl.multiple_of(h * 2 * N_KEYS, N_KEYS), N_KEYS), lanes]
        s2 = s_scr[pl.ds(pl.multiple_of(h * 2 * N_KEYS + N_KEYS, N_KEYS), N_KEYS), lanes]
        vals1, rank1, e1, _, _ = _top16(s1, key_iota, sub_iota)
        _, rank2, e2, v2_lo, v2_hi = _top16(s2, key_iota, sub_iota)
        counts, z = _select16(vals1, v2_lo, v2_hi, sub_iota)
        lrow = jnp.zeros((N_KEYS, LANES), F32)
        for a in range(TOPK):
            lrow = jnp.where(rank1 == float(a), counts[a], lrow)
        a1_ref[pl.ds(h, 1), :, lanes] = e1[None]
        lrow_ref[pl.ds(h, 1), :, lanes] = lrow[None]
        r2_ref[pl.ds(h, 1), :, lanes] = rank2[None]
        a2_ref[pl.ds(h, 1), :, lanes] = (e2 / z)[None]
        return carry

    lax.fori_loop(0, PEER_HEADS * n_lane_groups, body, 0)


def _route(hnt, wq_t, keys, tb=256):
    D, T = hnt.shape
    dq = wq_t.shape[0]
    spec = pl.BlockSpec((PEER_HEADS, N_KEYS, tb), lambda i: (0, 0, i))
    shape = jax.ShapeDtypeStruct((PEER_HEADS, N_KEYS, T), F32)
    return pl.pallas_call(
        functools.partial(_route_kernel, tb=tb),
        grid=(T // tb,),
        in_specs=[
            pl.BlockSpec((D, tb), lambda i: (0, i)),
            pl.BlockSpec((dq, D), lambda i: (0, 0)),
            pl.BlockSpec((2 * PEER_HEADS, N_KEYS, N_KEYS), lambda i: (0, 0, 0)),
        ],
        out_specs=[spec, spec, spec, spec],
        out_shape=[shape, shape, shape, shape],
        scratch_shapes=[pltpu.VMEM((dq, tb), BF16), pltpu.VMEM((dq, tb), F32)],
        compiler_params=_cparams(("arbitrary",)),
        name="peer_route",
    )(hnt, wq_t, keys)


def _peer_kernel(hnt_ref, u_ref, vt_ref, a1_ref, lrow_ref, r2_ref, a2_ref, h_ref,
                 y_ref, acc_scr, g_scr, *, et):
    j = pl.program_id(1)

    @pl.when(j == 0)
    def _():
        acc_scr[...] = jnp.zeros(acc_scr.shape, F32)

    ht = jnp.dot(u_ref[...], hnt_ref[...], preferred_element_type=F32)
    for r in range(et // N_KEYS):
        rows = slice(r * N_KEYS, (r + 1) * N_KEYS)
        w = None
        for h in range(PEER_HEADS):
            lb = lrow_ref[h, r:r + 1, :]
            ab = a1_ref[h, r:r + 1, :]
            wh = jnp.where(r2_ref[h] < lb, ab * a2_ref[h], 0.0)
            w = wh if w is None else w + wh
        x = ht[rows, :]
        gelu = 0.5 * x * (1.0 + lax.erf(x * (2.0 ** -0.5)))
        g_scr[rows, :] = (w * gelu).astype(BF16)
    acc_scr[...] += jnp.dot(vt_ref[...], g_scr[...], preferred_element_type=F32)

    @pl.when(j == pl.num_programs(1) - 1)
    def _():
        y_ref[...] = h_ref[...] + acc_scr[...].T


def _peer(hnt, u, vt, a1, lrow, r2, a2, h, tb=512, et=1024):
    D, T = hnt.shape
    E = u.shape[0]
    rows_per_step = et // N_KEYS
    sel_i1 = pl.BlockSpec((PEER_HEADS, rows_per_step, tb), lambda i, j: (0, j, i))
    sel_i2 = pl.BlockSpec((PEER_HEADS, N_KEYS, tb), lambda i, j: (0, 0, i))
    return pl.pallas_call(
        functools.partial(_peer_kernel, et=et),
        grid=(T // tb, E // et),
        in_specs=[
            pl.BlockSpec((D, tb), lambda i, j: (0, i)),
            pl.BlockSpec((et, D), lambda i, j: (j, 0)),
            pl.BlockSpec((D, et), lambda i, j: (0, j)),
            sel_i1, sel_i1, sel_i2, sel_i2,
            pl.BlockSpec((tb, D), lambda i, j: (i, 0)),
        ],
        out_specs=pl.BlockSpec((tb, D), lambda i, j: (i, 0)),
        out_shape=jax.ShapeDtypeStruct((T, D), F32),
        scratch_shapes=[pltpu.VMEM((D, tb), F32), pltpu.VMEM((et, tb), BF16)],
        compiler_params=_cparams(("arbitrary", "arbitrary")),
        name="peer_dense",
    )(hnt, u, vt, a1, lrow, r2, a2, h)


def _layer(h2, n_batch, seq, attn_g, w_in, b_forget, q_g, k_g, w_out, ffn_g,
           w_pq, sub_keys, u, v):
    T, D = h2.shape
    n_main = 7 * D_GROUP
    w_main = w_in[:, :n_main].astype(BF16)
    w_f = w_in[:, n_main:].T.astype(BF16)
    proj, gate, flog_t = _inproj(h2, attn_g.reshape(1, D), w_main, w_f,
                                 q_g.reshape(1, HEAD_DIM), k_g.reshape(1, HEAD_DIM))
    c = _forget_cumsum(flog_t, b_forget, seq)
    fox = _fox_attention(proj, gate, c.reshape(n_batch * N_HEADS, seq, 1),
                         c.reshape(n_batch * N_HEADS, 1, seq), n_batch, seq)
    sb = _sb_attention(proj, n_batch, seq)
    w_out_b = w_out.astype(BF16)
    h_mid, hnt = _outproj(fox, sb, h2, w_out_b[:D_GROUP], w_out_b[D_GROUP:],
                          ffn_g.reshape(1, D))
    keys = sub_keys.reshape(2 * PEER_HEADS, N_KEYS, -1).astype(BF16)
    a1, lrow, r2, a2 = _route(hnt, w_pq.T.astype(BF16), keys)
    return _peer(hnt, u.astype(BF16), v.T.astype(BF16), a1, lrow, r2, a2, h_mid)


def kernel(x, attn_norm_g, w_in, b_forget, q_norm_g, k_norm_g, w_out, ffn_norm_g,
           w_peer_q, peer_sub_keys, peer_u, peer_v):
    n_batch, seq, d_model = x.shape
    h = x.reshape(n_batch * seq, d_model)
    for l in range(attn_norm_g.shape[0]):
        h = _layer(h, n_batch, seq, attn_norm_g[l], w_in[l], b_forget[l], q_norm_g[l],
                   k_norm_g[l], w_out[l], ffn_norm_g[l], w_peer_q[l], peer_sub_keys[l],
                   peer_u[l], peer_v[l])
    return h.reshape(n_batch, seq, d_model)
```

```python
import functools

import jax
import jax.numpy as jnp
from jax import lax
from jax.experimental import pallas as pl
from jax.experimental.pallas import tpu as pltpu

F32 = jnp.float32
BF16 = jnp.bfloat16
EPS = 1e-6
HEAD_DIM = 128
N_HEADS = 8
D_GROUP = N_HEADS * HEAD_DIM
N_KEYS = 128
PEER_HEADS = 8
TOPK = 16
LANES = 128
SUBLANES = 8
NEG_INF = float("-inf")
NOT_SELECTED = 99.0
SB_UNDERFLOW_LOG = -105.0
VMEM_LIMIT = 56 * 1024 * 1024


def _cparams(sem):
    return pltpu.CompilerParams(dimension_semantics=sem, vmem_limit_bytes=VMEM_LIMIT)


def _log_sigmoid(z):
    return jnp.minimum(z, 0.0) - jnp.log1p(jnp.exp(-jnp.abs(z)))


def _inproj_kernel(x_ref, g_ref, w_ref, wf_ref, qg_ref, kg_ref,
                   out_ref, gate_ref, flog_ref, xn_scr):
    j = pl.program_id(1)

    @pl.when(j == 0)
    def _():
        x = x_ref[...]
        r = lax.rsqrt(jnp.mean(x * x, axis=-1, keepdims=True) + EPS)
        xn = ((x * r) * g_ref[...]).astype(BF16)
        xn_scr[...] = xn
        flog_ref[...] = lax.dot_general(wf_ref[...], xn, (((1,), (1,)), ((), ())),
                                        preferred_element_type=F32)

    p = jnp.dot(xn_scr[...], w_ref[...], preferred_element_type=F32)

    def head_norm(g):
        for h in range(N_HEADS):
            ph = p[:, h * HEAD_DIM:(h + 1) * HEAD_DIM]
            r = lax.rsqrt(jnp.mean(ph * ph, axis=-1, keepdims=True) + EPS)
            out_ref[0, :, h * HEAD_DIM:(h + 1) * HEAD_DIM] = ((ph * r) * g).astype(BF16)

    @pl.when(j == 0)
    def _():
        head_norm(qg_ref[...])

    @pl.when(j == 1)
    def _():
        head_norm(kg_ref[...])

    @pl.when(j == 3)
    def _():
        gate_ref[...] = jax.nn.sigmoid(p)
        out_ref[0] = p.astype(BF16)

    @pl.when((j == 2) | (j >= 4))
    def _():
        out_ref[0] = p.astype(BF16)


def _inproj(x2, g, w_main, w_f, qg, kg, tm=512):
    T, D = x2.shape
    n_groups = w_main.shape[1] // D_GROUP
    return pl.pallas_call(
        _inproj_kernel,
        grid=(T // tm, n_groups),
        in_specs=[
            pl.BlockSpec((tm, D), lambda i, j: (i, 0)),
            pl.BlockSpec((1, D), lambda i, j: (0, 0)),
            pl.BlockSpec((D, D_GROUP), lambda i, j: (0, j)),
            pl.BlockSpec((N_HEADS, D), lambda i, j: (0, 0)),
            pl.BlockSpec((1, HEAD_DIM), lambda i, j: (0, 0)),
            pl.BlockSpec((1, HEAD_DIM), lambda i, j: (0, 0)),
        ],
        out_specs=[
            pl.BlockSpec((1, tm, D_GROUP), lambda i, j: (j, i, 0)),
            pl.BlockSpec((tm, D_GROUP), lambda i, j: (i, 0)),
            pl.BlockSpec((N_HEADS, tm), lambda i, j: (0, i)),
        ],
        out_shape=[
            jax.ShapeDtypeStruct((n_groups, T, D_GROUP), BF16),
            jax.ShapeDtypeStruct((T, D_GROUP), F32),
            jax.ShapeDtypeStruct((N_HEADS, T), F32),
        ],
        scratch_shapes=[pltpu.VMEM((tm, D), BF16)],
        compiler_params=_cparams(("arbitrary", "arbitrary")),
        name="inproj",
    )(x2, g, w_main, w_f, qg, kg)


def _cumsum_kernel(flog_ref, b_ref, c_ref, *, seq):
    n_batch = flog_ref.shape[1] // seq
    lane = lax.broadcasted_iota(jnp.int32, (N_HEADS, seq), 1)
    for b in range(n_batch):
        x = _log_sigmoid(flog_ref[:, b * seq:(b + 1) * seq] + b_ref[...])
        sh = 1
        while sh < seq:
            x = x + jnp.where(lane >= sh, pltpu.roll(x, sh, axis=1), 0.0)
            sh *= 2
        c_ref[b * N_HEADS:(b + 1) * N_HEADS, :] = x


def _forget_cumsum(flog_t, b_forget, seq):
    T = flog_t.shape[1]
    n_batch = T // seq
    return pl.pallas_call(
        functools.partial(_cumsum_kernel, seq=seq),
        out_shape=jax.ShapeDtypeStruct((n_batch * N_HEADS, seq), F32),
        compiler_params=pltpu.CompilerParams(vmem_limit_bytes=VMEM_LIMIT),
        name="forget_cumsum",
    )(flog_t, b_forget.reshape(N_HEADS, 1))


def _fox_kernel(q_ref, k_ref, v_ref, gate_ref, cq_ref, ck_ref, o_ref,
                m_scr, l_scr, acc_scr, *, tq, scale):
    qi = pl.program_id(1)
    q = q_ref[0]
    cq = cq_ref[0]
    m_scr[...] = jnp.full(m_scr.shape, NEG_INF, F32)
    l_scr[...] = jnp.zeros(l_scr.shape, F32)
    acc_scr[...] = jnp.zeros(acc_scr.shape, F32)

    def step(kb, masked):
        off = pl.multiple_of(kb * tq, tq)
        k = k_ref[0, pl.ds(off, tq), :]
        v = v_ref[0, pl.ds(off, tq), :]
        ck = ck_ref[0, :, pl.ds(off, tq)]
        s = lax.dot_general(q, k, (((1,), (1,)), ((), ())), preferred_element_type=F32)
        s = s * scale + cq - ck
        if masked:
            row = lax.broadcasted_iota(jnp.int32, (tq, tq), 0)
            col = lax.broadcasted_iota(jnp.int32, (tq, tq), 1)
            s = jnp.where(col <= row, s, NEG_INF)
        m_prev = m_scr[...]
        m_new = jnp.maximum(m_prev, jnp.max(s, axis=1, keepdims=True))
        alpha = jnp.exp(m_prev - m_new)
        p = jnp.exp(s - m_new)
        l_scr[...] = alpha * l_scr[...] + jnp.sum(p, axis=1, keepdims=True)
        acc_scr[...] = alpha * acc_scr[...] + jnp.dot(p.astype(BF16), v,
                                                      preferred_element_type=F32)
        m_scr[...] = m_new

    def body(kb, carry):
        step(kb, False)
        return carry

    lax.fori_loop(0, qi, body, 0)
    step(qi, True)
    o_ref[...] = ((acc_scr[...] / l_scr[...]) * gate_ref[...]).astype(BF16)


def _fox_attention(proj, gate, c_col, c_row, n_batch, seq, tq=512):
    T = proj.shape[1]
    nq = seq // tq
    return pl.pallas_call(
        functools.partial(_fox_kernel, tq=tq, scale=HEAD_DIM ** -0.5),
        grid=(n_batch * N_HEADS, nq),
        in_specs=[
            pl.BlockSpec((1, tq, HEAD_DIM), lambda bh, i: (0, (bh // N_HEADS) * nq + i, bh % N_HEADS)),
            pl.BlockSpec((1, seq, HEAD_DIM), lambda bh, i: (1, bh // N_HEADS, bh % N_HEADS)),
            pl.BlockSpec((1, seq, HEAD_DIM), lambda bh, i: (2, bh // N_HEADS, bh % N_HEADS)),
            pl.BlockSpec((tq, HEAD_DIM), lambda bh, i: ((bh // N_HEADS) * nq + i, bh % N_HEADS)),
            pl.BlockSpec((1, tq, 1), lambda bh, i: (bh, i, 0)),
            pl.BlockSpec((1, 1, seq), lambda bh, i: (bh, 0, 0)),
        ],
        out_specs=pl.BlockSpec((tq, HEAD_DIM), lambda bh, i: ((bh // N_HEADS) * nq + i, bh % N_HEADS)),
        out_shape=jax.ShapeDtypeStruct((T, D_GROUP), BF16),
        scratch_shapes=[pltpu.VMEM((tq, 1), F32), pltpu.VMEM((tq, 1), F32),
                        pltpu.VMEM((tq, HEAD_DIM), F32)],
        compiler_params=_cparams(("arbitrary", "arbitrary")),
        name="fox_attention",
    )(proj, proj, proj, gate, c_col, c_row)


def _sb_kernel(q_ref, k_ref, v_ref, uo_ref, o_ref, acc_scr, carry_scr, *, tq, scale):
    qi = pl.program_id(1)
    q = q_ref[0]

    def group(g, valid):
        off = pl.multiple_of(g * tq, tq)
        k = k_ref[0, pl.ds(off, tq), :]
        v = v_ref[0, pl.ds(off, tq), :]
        z = lax.dot_general(q, k, (((1,), (1,)), ((), ())), preferred_element_type=F32) * scale
        lp = jnp.log(1.0 + jnp.exp(-jnp.abs(z)))
        t = jnp.minimum(z, 0.0)
        log_beta = t - lp
        log_1mb = (t - z) - lp
        if valid is not None:
            log_1mb = jnp.where(valid, log_1mb, 0.0)
        hi = log_1mb.astype(BF16)
        lo = (log_1mb - hi.astype(F32)).astype(BF16)
        cs = jnp.dot(jnp.concatenate([hi, lo], axis=1), uo_ref[...],
                     preferred_element_type=F32)
        return log_beta, cs[:, :tq], cs[:, tq:], v

    row = lax.broadcasted_iota(jnp.int32, (tq, tq), 0)
    col = lax.broadcasted_iota(jnp.int32, (tq, tq), 1)
    causal = col < row
    has_prev = qi > 0
    lb0, suf0, tot0, v0 = group(qi, causal)
    lb1, suf1, tot1, v1 = group(jnp.maximum(qi - 1, 0), has_prev)
    a0 = jnp.where(causal, jnp.exp(lb0 + suf0), 0.0)
    a1 = jnp.where(has_prev, jnp.exp(lb1 + suf1 + tot0), 0.0)
    acc_scr[...] = (jnp.dot(a0.astype(BF16), v0, preferred_element_type=F32)
                    + jnp.dot(a1.astype(BF16), v1, preferred_element_type=F32))
    carry_scr[...] = tot0 + tot1

    def live():
        return jnp.max(carry_scr[...]) > SB_UNDERFLOW_LOG

    def body(state):
        g, _ = state
        lb, suf, tot, v = group(g, None)
        carry = carry_scr[...]
        a = jnp.exp(lb + suf + carry)
        acc_scr[...] += jnp.dot(a.astype(BF16), v, preferred_element_type=F32)
        carry_scr[...] = carry + tot
        return g - 1, live()

    lax.while_loop(lambda state: (state[0] >= 0) & state[1], body, (qi - 2, live()))
    o_ref[...] = acc_scr[...].astype(BF16)


def _suffix_matrix(n):
    j = jnp.arange(2 * n)[:, None] % n
    s = jnp.arange(n)[None, :]
    tri = (j > s).astype(BF16)
    return jnp.concatenate([tri, jnp.ones((2 * n, n), BF16)], axis=1)


def _sb_attention(proj, n_batch, seq, tq=256):
    T = proj.shape[1]
    nq = seq // tq
    return pl.pallas_call(
        functools.partial(_sb_kernel, tq=tq, scale=HEAD_DIM ** -0.5),
        grid=(n_batch * N_HEADS, nq),
        in_specs=[
            pl.BlockSpec((1, tq, HEAD_DIM), lambda bh, i: (4, (bh // N_HEADS) * nq + i, bh % N_HEADS)),
            pl.BlockSpec((1, seq, HEAD_DIM), lambda bh, i: (5, bh // N_HEADS, bh % N_HEADS)),
            pl.BlockSpec((1, seq, HEAD_DIM), lambda bh, i: (6, bh // N_HEADS, bh % N_HEADS)),
            pl.BlockSpec((2 * tq, 2 * tq), lambda bh, i: (0, 0)),
        ],
        out_specs=pl.BlockSpec((tq, HEAD_DIM), lambda bh, i: ((bh // N_HEADS) * nq + i, bh % N_HEADS)),
        out_shape=jax.ShapeDtypeStruct((T, D_GROUP), BF16),
        scratch_shapes=[pltpu.VMEM((tq, HEAD_DIM), F32), pltpu.VMEM((tq, tq), F32)],
        compiler_params=_cparams(("arbitrary", "arbitrary")),
        name="sb_attention",
    )(proj, proj, proj, _suffix_matrix(tq))


def _outproj_kernel(fox_ref, sb_ref, x_ref, wo1_ref, wo2_ref, g_ref, h_ref, hnt_ref):
    h = (x_ref[...]
         + jnp.dot(fox_ref[...], wo1_ref[...], preferred_element_type=F32)
         + jnp.dot(sb_ref[...], wo2_ref[...], preferred_element_type=F32))
    h_ref[...] = h
    r = lax.rsqrt(jnp.mean(h * h, axis=-1, keepdims=True) + EPS)
    hn = (h * r) * g_ref[...]
    hnt_ref[...] = hn.T.astype(BF16)


def _outproj(fox, sb, x2, wo1, wo2, g, tm=512):
    T, D = x2.shape
    return pl.pallas_call(
        _outproj_kernel,
        grid=(T // tm,),
        in_specs=[
            pl.BlockSpec((tm, D_GROUP), lambda i: (i, 0)),
            pl.BlockSpec((tm, D_GROUP), lambda i: (i, 0)),
            pl.BlockSpec((tm, D), lambda i: (i, 0)),
            pl.BlockSpec((D_GROUP, D), lambda i: (0, 0)),
            pl.BlockSpec((D_GROUP, D), lambda i: (0, 0)),
            pl.BlockSpec((1, D), lambda i: (0, 0)),
        ],
        out_specs=[
            pl.BlockSpec((tm, D), lambda i: (i, 0)),
            pl.BlockSpec((D, tm), lambda i: (0, i)),
        ],
        out_shape=[jax.ShapeDtypeStruct((T, D), F32), jax.ShapeDtypeStruct((D, T), BF16)],
        compiler_params=_cparams(("arbitrary",)),
        name="outproj",
    )(fox, sb, x2, wo1, wo2, g)


def _max0(x):
    return jnp.max(x, axis=0, keepdims=True)


def _min0(x):
    return jnp.min(x, axis=0, keepdims=True)


def _top16(s, key_iota, sub_iota):
    rank = jnp.full(s.shape, NOT_SELECTED, F32)
    ew = jnp.zeros(s.shape, F32)
    v_lo = jnp.zeros((SUBLANES, LANES), F32)
    v_hi = jnp.zeros((SUBLANES, LANES), F32)
    vals = []
    for it in range(TOPK):
        m = _max0(s)
        first = _min0(jnp.where(s == m, key_iota, float(N_KEYS)))
        hit = key_iota == first
        s = jnp.where(hit, NEG_INF, s)
        rank = jnp.where(hit, float(it), rank)
        ew = jnp.where(hit, jnp.exp(m - vals[0]) if vals else 1.0, ew)
        if it < SUBLANES:
            v_lo = jnp.where(sub_iota == float(it), m, v_lo)
        else:
            v_hi = jnp.where(sub_iota == float(it - SUBLANES), m, v_hi)
        vals.append(m)
    return vals, rank, ew, v_lo, v_hi


def _select16(vals1, v2_lo, v2_hi, sub_iota):
    slabs, pos, n_invalid = [], [], []
    for a in range(TOPK):
        nb = min(TOPK // (a + 1), SUBLANES)
        c = vals1[a] + v2_lo
        if nb < SUBLANES:
            c = jnp.where(sub_iota < float(nb), c, NEG_INF)
        slabs.append(c)
        pos.append(sub_iota + float(TOPK * a))
        n_invalid.append(float(SUBLANES - nb))
    slabs.append(vals1[0] + v2_hi)
    pos.append(sub_iota + float(SUBLANES))
    n_invalid.append(0.0)

    def tree(op, xs):
        xs = list(xs)
        while len(xs) > 1:
            xs = [op(xs[i], xs[i + 1]) if i + 1 < len(xs) else xs[i]
                  for i in range(0, len(xs), 2)]
        return xs[0]

    z = None
    c_max = None
    for it in range(TOPK):
        m = _max0(tree(jnp.maximum, slabs))
        first = _min0(tree(jnp.minimum,
                           [jnp.where(c == m, p, 999.0) for c, p in zip(slabs, pos)]))
        slabs = [jnp.where(p == first, NEG_INF, c) for c, p in zip(slabs, pos)]
        if it == 0:
            c_max = m
            z = jnp.ones_like(m)
        else:
            z = z + jnp.exp(m - c_max)
    counts = []
    for i, c in enumerate(slabs):
        taken = jnp.sum(jnp.where(c == NEG_INF, 1.0, 0.0), axis=0, keepdims=True)
        counts.append(taken - n_invalid[i])
    counts[0] = counts[0] + counts.pop()
    return counts, z


def _route_kernel(hnt_ref, wq_ref, keys_ref, a1_ref, lrow_ref, r2_ref, a2_ref,
                  q_scr, s_scr, *, tb):
    qt = jnp.dot(wq_ref[...], hnt_ref[...], preferred_element_type=F32)
    q_scr[...] = qt.astype(BF16)
    for hc in range(2 * PEER_HEADS):
        rows = slice(hc * N_KEYS, (hc + 1) * N_KEYS)
        s_scr[rows, :] = jnp.dot(keys_ref[hc], q_scr[rows, :], preferred_element_type=F32)

    n_lane_groups = tb // LANES
    key_iota = lax.broadcasted_iota(jnp.int32, (N_KEYS, LANES), 0).astype(F32)
    sub_iota = lax.broadcasted_iota(jnp.int32, (SUBLANES, LANES), 0).astype(F32)

    def body(idx, carry):
        h = idx // n_lane_groups
        lanes = pl.ds(pl.multiple_of((idx % n_lane_groups) * LANES, LANES), LANES)
        s1 = s_scr[pl.ds(pl.multiple_of(h * 2 * N_KEYS, N_KEYS), N_KEYS), lanes]
        s2 = s_scr[pl.ds(pl.multiple_of(h * 2 * N_KEYS + N_KEYS, N_KEYS), N_KEYS), lanes]
        vals1, rank1, e1, _, _ = _top16(s1, key_iota, sub_iota)
        _, rank2, e2, v2_lo, v2_hi = _top16(s2, key_iota, sub_iota)
        counts, z = _select16(vals1, v2_lo, v2_hi, sub_iota)
        lrow = jnp.zeros((N_KEYS, LANES), F32)
        for a in range(TOPK):
            lrow = jnp.where(rank1 == float(a), counts[a], lrow)
        a1_ref[pl.ds(h, 1), :, lanes] = e1[None]
        lrow_ref[pl.ds(h, 1), :, lanes] = lrow[None]
        r2_ref[pl.ds(h, 1), :, lanes] = rank2.astype(BF16)[None]
        a2_ref[pl.ds(h, 1), :, lanes] = (e2 / z).astype(BF16)[None]
        return carry

    lax.fori_loop(0, PEER_HEADS * n_lane_groups, body, 0)


def _route(hnt, wq_t, keys, tb=256):
    D, T = hnt.shape
    dq = wq_t.shape[0]
    spec = pl.BlockSpec((PEER_HEADS, N_KEYS, tb), lambda i: (0, 0, i))
    shape = jax.ShapeDtypeStruct((PEER_HEADS, N_KEYS, T), F32)
    shape_b = jax.ShapeDtypeStruct((PEER_HEADS, N_KEYS, T), BF16)
    return pl.pallas_call(
        functools.partial(_route_kernel, tb=tb),
        grid=(T // tb,),
        in_specs=[
            pl.BlockSpec((D, tb), lambda i: (0, i)),
            pl.BlockSpec((dq, D), lambda i: (0, 0)),
            pl.BlockSpec((2 * PEER_HEADS, N_KEYS, N_KEYS), lambda i: (0, 0, 0)),
        ],
        out_specs=[spec, spec, spec, spec],
        out_shape=[shape, shape, shape_b, shape_b],
        scratch_shapes=[pltpu.VMEM((dq, tb), BF16), pltpu.VMEM((dq, tb), F32)],
        compiler_params=_cparams(("arbitrary",)),
        name="peer_route",
    )(hnt, wq_t, keys)


def _peer_kernel(hnt_ref, u_ref, vt_ref, a1_ref, lrow_ref, r2_ref, a2_ref, h_ref,
                 y_ref, acc_scr, g_scr, *, et, n_split):
    j = pl.program_id(1)
    tb = hnt_ref.shape[1]
    part = et // n_split
    pack = 2 * SUBLANES

    @pl.when(j == 0)
    def _():
        acc_scr[...] = jnp.zeros(acc_scr.shape, F32)

    def gate_rows(ht, s):
        for r in range(part // N_KEYS):
            i1 = s * (part // N_KEYS) + r
            lb = [jnp.broadcast_to(lrow_ref[h, i1:i1 + 1, :], (pack, tb)).astype(BF16)
                  for h in range(PEER_HEADS)]
            ab = [jnp.broadcast_to(a1_ref[h, i1:i1 + 1, :], (pack, tb)).astype(BF16)
                  for h in range(PEER_HEADS)]
            for c in range(N_KEYS // pack):
                i2 = slice(c * pack, (c + 1) * pack)
                w = None
                for h in range(PEER_HEADS):
                    wh = jnp.where(r2_ref[h, i2, :] < lb[h], ab[h] * a2_ref[h, i2, :],
                                   jnp.zeros((), BF16))
                    w = wh if w is None else w + wh
                x = ht[r * N_KEYS + c * pack:r * N_KEYS + (c + 1) * pack, :]
                gelu = 0.5 * x * (1.0 + lax.erf(x * (2.0 ** -0.5)))
                g_scr[s, r * N_KEYS + c * pack:r * N_KEYS + (c + 1) * pack, :] = (
                    w * gelu.astype(BF16))

    hnt = hnt_ref[...]
    hts = [jnp.dot(u_ref[s * part:(s + 1) * part, :], hnt, preferred_element_type=F32)
           for s in range(n_split)]
    for s in range(n_split):
        gate_rows(hts[s], s)
        acc_scr[...] += jnp.dot(vt_ref[:, s * part:(s + 1) * part], g_scr[s],
                                preferred_element_type=F32)

    @pl.when(j == pl.num_programs(1) - 1)
    def _():
        y_ref[...] = h_ref[...] + acc_scr[...].T


def _peer(hnt, u, vt, a1, lrow, r2, a2, h, tb=512, et=1024, n_split=2):
    D, T = hnt.shape
    E = u.shape[0]
    rows_per_step = et // N_KEYS
    sel_i1 = pl.BlockSpec((PEER_HEADS, rows_per_step, tb), lambda i, j: (0, j, i))
    sel_i2 = pl.BlockSpec((PEER_HEADS, N_KEYS, tb), lambda i, j: (0, 0, i))
    return pl.pallas_call(
        functools.partial(_peer_kernel, et=et, n_split=n_split),
        grid=(T // tb, E // et),
        in_specs=[
            pl.BlockSpec((D, tb), lambda i, j: (0, i)),
            pl.BlockSpec((et, D), lambda i, j: (j, 0)),
            pl.BlockSpec((D, et), lambda i, j: (0, j)),
            sel_i1, sel_i1, sel_i2, sel_i2,
            pl.BlockSpec((tb, D), lambda i, j: (i, 0)),
        ],
        out_specs=pl.BlockSpec((tb, D), lambda i, j: (i, 0)),
        out_shape=jax.ShapeDtypeStruct((T, D), F32),
        scratch_shapes=[pltpu.VMEM((D, tb), F32),
                        pltpu.VMEM((n_split, et // n_split, tb), BF16)],
        compiler_params=_cparams(("arbitrary", "arbitrary")),
        name="peer_dense",
    )(hnt, u, vt, a1, lrow, r2, a2, h)


def _layer(h2, n_batch, seq, attn_g, w_in, b_forget, q_g, k_g, w_out, ffn_g,
           w_pq, sub_keys, u, v):
    T, D = h2.shape
    n_main = 7 * D_GROUP
    w_main = w_in[:, :n_main].astype(BF16)
    w_f = w_in[:, n_main:].T.astype(BF16)
    proj, gate, flog_t = _inproj(h2, attn_g.reshape(1, D), w_main, w_f,
                                 q_g.reshape(1, HEAD_DIM), k_g.reshape(1, HEAD_DIM))
    c = _forget_cumsum(flog_t, b_forget, seq)
    fox = _fox_attention(proj, gate, c.reshape(n_batch * N_HEADS, seq, 1),
                         c.reshape(n_batch * N_HEADS, 1, seq), n_batch, seq)
    sb = _sb_attention(proj, n_batch, seq)
    w_out_b = w_out.astype(BF16)
    h_mid, hnt = _outproj(fox, sb, h2, w_out_b[:D_GROUP], w_out_b[D_GROUP:],
                          ffn_g.reshape(1, D))
    keys = sub_keys.reshape(2 * PEER_HEADS, N_KEYS, -1).astype(BF16)
    a1, lrow, r2, a2 = _route(hnt, w_pq.T.astype(BF16), keys)
    return _peer(hnt, u.astype(BF16), v.T.astype(BF16), a1, lrow, r2, a2, h_mid)


def kernel(x, attn_norm_g, w_in, b_forget, q_norm_g, k_norm_g, w_out, ffn_norm_g,
           w_peer_q, peer_sub_keys, peer_u, peer_v):
    n_batch, seq, d_model = x.shape
    h = x.reshape(n_batch * seq, d_model)
    for l in range(attn_norm_g.shape[0]):
        h = _layer(h, n_batch, seq, attn_norm_g[l], w_in[l], b_forget[l], q_norm_g[l],
                   k_norm_g[l], w_out[l], ffn_norm_g[l], w_peer_q[l], peer_sub_keys[l],
                   peer_u[l], peer_v[l])
    return h.reshape(n_batch, seq, d_model)
```

```python
import functools

import jax
import jax.numpy as jnp
from jax import lax
from jax.experimental import pallas as pl
from jax.experimental.pallas import tpu as pltpu

F32 = jnp.float32
BF16 = jnp.bfloat16
EPS = 1e-6
HEAD_DIM = 128
N_HEADS = 8
D_GROUP = N_HEADS * HEAD_DIM
N_KEYS = 128
PEER_HEADS = 8
TOPK = 16
LANES = 128
SUBLANES = 8
NEG_INF = float("-inf")
NOT_SELECTED = 99.0
SB_UNDERFLOW_LOG = -105.0
VMEM_LIMIT = 56 * 1024 * 1024
PEER_ET = 1024


def _cparams(sem):
    return pltpu.CompilerParams(dimension_semantics=sem, vmem_limit_bytes=VMEM_LIMIT)


def _log_sigmoid(z):
    return jnp.minimum(z, 0.0) - jnp.log1p(jnp.exp(-jnp.abs(z)))


def _inproj_kernel(x_ref, g_ref, w_ref, wf_ref, qg_ref, kg_ref,
                   out_ref, gate_ref, flog_ref, xn_scr):
    j = pl.program_id(1)

    @pl.when(j == 0)
    def _():
        x = x_ref[...]
        r = lax.rsqrt(jnp.mean(x * x, axis=-1, keepdims=True) + EPS)
        xn = ((x * r) * g_ref[...]).astype(BF16)
        xn_scr[...] = xn
        flog_ref[...] = lax.dot_general(wf_ref[...], xn, (((1,), (1,)), ((), ())),
                                        preferred_element_type=F32)

    p = jnp.dot(xn_scr[...], w_ref[...], preferred_element_type=F32)

    def head_norm(g):
        for h in range(N_HEADS):
            ph = p[:, h * HEAD_DIM:(h + 1) * HEAD_DIM]
            r = lax.rsqrt(jnp.mean(ph * ph, axis=-1, keepdims=True) + EPS)
            out_ref[0, :, h * HEAD_DIM:(h + 1) * HEAD_DIM] = ((ph * r) * g).astype(BF16)

    @pl.when(j == 0)
    def _():
        head_norm(qg_ref[...])

    @pl.when(j == 1)
    def _():
        head_norm(kg_ref[...])

    @pl.when(j == 3)
    def _():
        gate_ref[...] = jax.nn.sigmoid(p)
        out_ref[0] = p.astype(BF16)

    @pl.when((j == 2) | (j >= 4))
    def _():
        out_ref[0] = p.astype(BF16)


def _inproj(x2, g, w_main, w_f, qg, kg, tm=512):
    T, D = x2.shape
    n_groups = w_main.shape[1] // D_GROUP
    return pl.pallas_call(
        _inproj_kernel,
        grid=(T // tm, n_groups),
        in_specs=[
            pl.BlockSpec((tm, D), lambda i, j: (i, 0)),
            pl.BlockSpec((1, D), lambda i, j: (0, 0)),
            pl.BlockSpec((D, D_GROUP), lambda i, j: (0, j)),
            pl.BlockSpec((N_HEADS, D), lambda i, j: (0, 0)),
            pl.BlockSpec((1, HEAD_DIM), lambda i, j: (0, 0)),
            pl.BlockSpec((1, HEAD_DIM), lambda i, j: (0, 0)),
        ],
        out_specs=[
            pl.BlockSpec((1, tm, D_GROUP), lambda i, j: (j, i, 0)),
            pl.BlockSpec((tm, D_GROUP), lambda i, j: (i, 0)),
            pl.BlockSpec((N_HEADS, tm), lambda i, j: (0, i)),
        ],
        out_shape=[
            jax.ShapeDtypeStruct((n_groups, T, D_GROUP), BF16),
            jax.ShapeDtypeStruct((T, D_GROUP), F32),
            jax.ShapeDtypeStruct((N_HEADS, T), F32),
        ],
        scratch_shapes=[pltpu.VMEM((tm, D), BF16)],
        compiler_params=_cparams(("arbitrary", "arbitrary")),
        name="inproj",
    )(x2, g, w_main, w_f, qg, kg)


def _cumsum_kernel(flog_ref, b_ref, c_ref, *, seq):
    n_batch = flog_ref.shape[1] // seq
    lane = lax.broadcasted_iota(jnp.int32, (N_HEADS, seq), 1)
    for b in range(n_batch):
        x = _log_sigmoid(flog_ref[:, b * seq:(b + 1) * seq] + b_ref[...])
        sh = 1
        while sh < seq:
            x = x + jnp.where(lane >= sh, pltpu.roll(x, sh, axis=1), 0.0)
            sh *= 2
        c_ref[b * N_HEADS:(b + 1) * N_HEADS, :] = x


def _forget_cumsum(flog_t, b_forget, seq):
    T = flog_t.shape[1]
    n_batch = T // seq
    return pl.pallas_call(
        functools.partial(_cumsum_kernel, seq=seq),
        out_shape=jax.ShapeDtypeStruct((n_batch * N_HEADS, seq), F32),
        compiler_params=pltpu.CompilerParams(vmem_limit_bytes=VMEM_LIMIT),
        name="forget_cumsum",
    )(flog_t, b_forget.reshape(N_HEADS, 1))


def _fox_kernel(q_ref, k_ref, v_ref, gate_ref, cq_ref, ck_ref, o_ref,
                m_scr, acc_scr, *, tq, tk, scale):
    qi = pl.program_id(1)
    q = q_ref[0]
    cq = cq_ref[0]
    m_scr[...] = jnp.full(m_scr.shape, NEG_INF, F32)
    acc_scr[...] = jnp.zeros(acc_scr.shape, F32)
    ones = jnp.ones((tk, HEAD_DIM), BF16)
    n_sub = tq // tk

    def block(kb, diagonal):
        scores = []
        for d in range(n_sub):
            off = pl.multiple_of(kb * tq + d * tk, tk)
            k = k_ref[0, pl.ds(off, tk), :]
            ck = ck_ref[0, :, pl.ds(off, tk)]
            s = lax.dot_general(q, k, (((1,), (1,)), ((), ())), preferred_element_type=F32)
            s = s * scale + cq - ck
            if diagonal:
                row = lax.broadcasted_iota(jnp.int32, (tq, tk), 0)
                col = lax.broadcasted_iota(jnp.int32, (tq, tk), 1)
                s = jnp.where(col + d * tk <= row, s, NEG_INF)
            scores.append(s)
        m = m_scr[...]
        decay = None
        new = None
        for d in range(n_sub):
            off = pl.multiple_of(kb * tq + d * tk, tk)
            v = v_ref[0, pl.ds(off, tk), :]
            m_new = jnp.maximum(m, jnp.max(scores[d], axis=1, keepdims=True))
            alpha = jnp.exp(m - m_new)
            p = jnp.exp(scores[d] - m_new)
            pv = jnp.dot(p.astype(BF16), jnp.concatenate([v, ones], axis=1),
                         preferred_element_type=F32)
            new = pv if new is None else new * alpha + pv
            decay = alpha if decay is None else decay * alpha
            m = m_new
        acc_scr[...] = acc_scr[...] * decay + new
        m_scr[...] = m

    def body(kb, carry):
        block(kb, False)
        return carry

    block(qi, True)
    lax.fori_loop(0, qi, body, 0)
    acc = acc_scr[...]
    o_ref[...] = ((acc[:, :HEAD_DIM] / acc[:, HEAD_DIM:]) * gate_ref[...]).astype(BF16)


def _fox_attention(proj, gate, c_col, c_row, n_batch, seq, tq=512, tk=512):
    T = proj.shape[1]
    nq = seq // tq
    return pl.pallas_call(
        functools.partial(_fox_kernel, tq=tq, tk=tk, scale=HEAD_DIM ** -0.5),
        grid=(n_batch * N_HEADS, nq),
        in_specs=[
            pl.BlockSpec((1, tq, HEAD_DIM), lambda bh, i: (0, (bh // N_HEADS) * nq + i, bh % N_HEADS)),
            pl.BlockSpec((1, seq, HEAD_DIM), lambda bh, i: (1, bh // N_HEADS, bh % N_HEADS)),
            pl.BlockSpec((1, seq, HEAD_DIM), lambda bh, i: (2, bh // N_HEADS, bh % N_HEADS)),
            pl.BlockSpec((tq, HEAD_DIM), lambda bh, i: ((bh // N_HEADS) * nq + i, bh % N_HEADS)),
            pl.BlockSpec((1, tq, 1), lambda bh, i: (bh, i, 0)),
            pl.BlockSpec((1, 1, seq), lambda bh, i: (bh, 0, 0)),
        ],
        out_specs=pl.BlockSpec((tq, HEAD_DIM), lambda bh, i: ((bh // N_HEADS) * nq + i, bh % N_HEADS)),
        out_shape=jax.ShapeDtypeStruct((T, D_GROUP), BF16),
        scratch_shapes=[pltpu.VMEM((tq, 1), F32), pltpu.VMEM((tq, 2 * HEAD_DIM), F32)],
        compiler_params=_cparams(("arbitrary", "arbitrary")),
        name="fox_attention",
    )(proj, proj, proj, gate, c_col, c_row)


def _sb_kernel(q_ref, k_ref, v_ref, uo_ref, o_ref, acc_scr, carry_scr, *, tq, scale):
    qi = pl.program_id(1)
    q = q_ref[0]

    def group(g, valid):
        off = pl.multiple_of(g * tq, tq)
        k = k_ref[0, pl.ds(off, tq), :]
        v = v_ref[0, pl.ds(off, tq), :]
        z = lax.dot_general(q, k, (((1,), (1,)), ((), ())), preferred_element_type=F32) * scale
        lp = jnp.log(1.0 + jnp.exp(-jnp.abs(z)))
        t = jnp.minimum(z, 0.0)
        log_beta = t - lp
        log_1mb = (t - z) - lp
        if valid is not None:
            log_1mb = jnp.where(valid, log_1mb, 0.0)
        hi = log_1mb.astype(BF16)
        lo = (log_1mb - hi.astype(F32)).astype(BF16)
        cs = jnp.dot(jnp.concatenate([hi, lo], axis=1), uo_ref[...],
                     preferred_element_type=F32)
        return log_beta, cs[:, :tq], cs[:, tq:], v

    row = lax.broadcasted_iota(jnp.int32, (tq, tq), 0)
    col = lax.broadcasted_iota(jnp.int32, (tq, tq), 1)
    causal = col < row
    has_prev = qi > 0
    lb0, suf0, tot0, v0 = group(qi, causal)
    lb1, suf1, tot1, v1 = group(jnp.maximum(qi - 1, 0), has_prev)
    a0 = jnp.where(causal, jnp.exp(lb0 + suf0), 0.0)
    a1 = jnp.where(has_prev, jnp.exp(lb1 + suf1 + tot0), 0.0)
    acc_scr[...] = (jnp.dot(a0.astype(BF16), v0, preferred_element_type=F32)
                    + jnp.dot(a1.astype(BF16), v1, preferred_element_type=F32))
    carry_scr[...] = tot0 + tot1

    def live():
        return jnp.max(carry_scr[...]) > SB_UNDERFLOW_LOG

    def body(state):
        g, _ = state
        lb, suf, tot, v = group(g, None)
        carry = carry_scr[...]
        a = jnp.exp(lb + suf + carry)
        acc_scr[...] += jnp.dot(a.astype(BF16), v, preferred_element_type=F32)
        carry_scr[...] = carry + tot
        return g - 1, live()

    lax.while_loop(lambda state: (state[0] >= 0) & state[1], body, (qi - 2, live()))
    o_ref[...] = acc_scr[...].astype(BF16)


def _suffix_matrix(n):
    j = jnp.arange(2 * n)[:, None] % n
    s = jnp.arange(n)[None, :]
    tri = (j > s).astype(BF16)
    return jnp.concatenate([tri, jnp.ones((2 * n, n), BF16)], axis=1)


def _sb_attention(proj, n_batch, seq, tq=256):
    T = proj.shape[1]
    nq = seq // tq
    return pl.pallas_call(
        functools.partial(_sb_kernel, tq=tq, scale=HEAD_DIM ** -0.5),
        grid=(n_batch * N_HEADS, nq),
        in_specs=[
            pl.BlockSpec((1, tq, HEAD_DIM), lambda bh, i: (4, (bh // N_HEADS) * nq + i, bh % N_HEADS)),
            pl.BlockSpec((1, seq, HEAD_DIM), lambda bh, i: (5, bh // N_HEADS, bh % N_HEADS)),
            pl.BlockSpec((1, seq, HEAD_DIM), lambda bh, i: (6, bh // N_HEADS, bh % N_HEADS)),
            pl.BlockSpec((2 * tq, 2 * tq), lambda bh, i: (0, 0)),
        ],
        out_specs=pl.BlockSpec((tq, HEAD_DIM), lambda bh, i: ((bh // N_HEADS) * nq + i, bh % N_HEADS)),
        out_shape=jax.ShapeDtypeStruct((T, D_GROUP), BF16),
        scratch_shapes=[pltpu.VMEM((tq, HEAD_DIM), F32), pltpu.VMEM((tq, tq), F32)],
        compiler_params=_cparams(("arbitrary", "arbitrary")),
        name="sb_attention",
    )(proj, proj, proj, _suffix_matrix(tq))


def _outproj_kernel(fox_ref, sb_ref, x_ref, wo1_ref, wo2_ref, g_ref, h_ref, hnt_ref):
    h = (x_ref[...]
         + jnp.dot(fox_ref[...], wo1_ref[...], preferred_element_type=F32)
         + jnp.dot(sb_ref[...], wo2_ref[...], preferred_element_type=F32))
    h_ref[...] = h
    r = lax.rsqrt(jnp.mean(h * h, axis=-1, keepdims=True) + EPS)
    hn = (h * r) * g_ref[...]
    hnt_ref[...] = hn.T.astype(BF16)


def _outproj(fox, sb, x2, wo1, wo2, g, tm=512):
    T, D = x2.shape
    return pl.pallas_call(
        _outproj_kernel,
        grid=(T // tm,),
        in_specs=[
            pl.BlockSpec((tm, D_GROUP), lambda i: (i, 0)),
            pl.BlockSpec((tm, D_GROUP), lambda i: (i, 0)),
            pl.BlockSpec((tm, D), lambda i: (i, 0)),
            pl.BlockSpec((D_GROUP, D), lambda i: (0, 0)),
            pl.BlockSpec((D_GROUP, D), lambda i: (0, 0)),
            pl.BlockSpec((1, D), lambda i: (0, 0)),
        ],
        out_specs=[
            pl.BlockSpec((tm, D), lambda i: (i, 0)),
            pl.BlockSpec((D, tm), lambda i: (0, i)),
        ],
        out_shape=[jax.ShapeDtypeStruct((T, D), F32), jax.ShapeDtypeStruct((D, T), BF16)],
        compiler_params=_cparams(("arbitrary",)),
        name="outproj",
    )(fox, sb, x2, wo1, wo2, g)


def _max0(x):
    return jnp.max(x, axis=0, keepdims=True)


def _min0(x):
    return jnp.min(x, axis=0, keepdims=True)


def _top16(s, key_iota, sub_iota):
    rank = jnp.full(s.shape, NOT_SELECTED, F32)
    ew = jnp.zeros(s.shape, F32)
    v_lo = jnp.zeros((SUBLANES, LANES), F32)
    v_hi = jnp.zeros((SUBLANES, LANES), F32)
    vals = []
    for it in range(TOPK):
        m = _max0(s)
        first = _min0(jnp.where(s == m, key_iota, float(N_KEYS)))
        hit = key_iota == first
        s = jnp.where(hit, NEG_INF, s)
        rank = jnp.where(hit, float(it), rank)
        ew = jnp.where(hit, jnp.exp(m - vals[0]) if vals else 1.0, ew)
        if it < SUBLANES:
            v_lo = jnp.where(sub_iota == float(it), m, v_lo)
        else:
            v_hi = jnp.where(sub_iota == float(it - SUBLANES), m, v_hi)
        vals.append(m)
    return vals, rank, ew, v_lo, v_hi


def _top16_distinct(s, sub_iota):
    s0 = s
    rank = jnp.full(s.shape, NOT_SELECTED, F32)
    v_lo = jnp.zeros((SUBLANES, LANES), F32)
    v_hi = jnp.zeros((SUBLANES, LANES), F32)
    vals = []
    for it in range(TOPK):
        m = _max0(s)
        hit = s == m
        s = jnp.where(hit, NEG_INF, s)
        rank = jnp.where(hit, float(it), rank)
        if it < SUBLANES:
            v_lo = jnp.where(sub_iota == float(it), m, v_lo)
        else:
            v_hi = jnp.where(sub_iota == float(it - SUBLANES), m, v_hi)
        vals.append(m)
    taken = s == NEG_INF
    ew = jnp.where(taken, jnp.exp(s0 - vals[0]), 0.0)
    n_taken = jnp.sum(jnp.where(taken, 1.0, 0.0), axis=0, keepdims=True)
    ok = jnp.max(jnp.abs(n_taken - float(TOPK))) == 0.0
    return vals, rank, ew, v_lo, v_hi, ok


def _select16(vals1, v2_lo, v2_hi, sub_iota, tie_break=True):
    slabs, pos, n_invalid = [], [], []
    for a in range(TOPK):
        nb = min(TOPK // (a + 1), SUBLANES)
        c = vals1[a] + v2_lo
        if nb < SUBLANES:
            c = jnp.where(sub_iota < float(nb), c, NEG_INF)
        slabs.append(c)
        pos.append(sub_iota + float(TOPK * a))
        n_invalid.append(float(SUBLANES - nb))
    slabs.append(vals1[0] + v2_hi)
    pos.append(sub_iota + float(SUBLANES))
    n_invalid.append(0.0)

    def tree(op, xs):
        xs = list(xs)
        while len(xs) > 1:
            xs = [op(xs[i], xs[i + 1]) if i + 1 < len(xs) else xs[i]
                  for i in range(0, len(xs), 2)]
        return xs[0]

    z = None
    c_max = None
    for it in range(TOPK):
        m = _max0(tree(jnp.maximum, slabs))
        if tie_break:
            first = _min0(tree(jnp.minimum,
                               [jnp.where(c == m, p, 999.0) for c, p in zip(slabs, pos)]))
            slabs = [jnp.where(p == first, NEG_INF, c) for c, p in zip(slabs, pos)]
        else:
            slabs = [jnp.where(c == m, NEG_INF, c) for c in slabs]
        if it == 0:
            c_max = m
            z = jnp.ones_like(m)
        else:
            z = z + jnp.exp(m - c_max)
    counts = []
    for i, c in enumerate(slabs):
        taken = jnp.sum(jnp.where(c == NEG_INF, 1.0, 0.0), axis=0, keepdims=True)
        counts.append(taken - n_invalid[i])
    counts[0] = counts[0] + counts.pop()
    return counts, z


def _route_kernel(hnt_ref, wq_ref, keys_ref, a1_ref, lrow_ref, r2_ref, a2_ref,
                  q_scr, s_scr, *, tb):
    qt = jnp.dot(wq_ref[...], hnt_ref[...], preferred_element_type=F32)
    q_scr[...] = qt.astype(BF16)
    for hc in range(2 * PEER_HEADS):
        rows = slice(hc * N_KEYS, (hc + 1) * N_KEYS)
        s_scr[rows, :] = jnp.dot(keys_ref[hc], q_scr[rows, :], preferred_element_type=F32)

    n_lane_groups = tb // LANES
    key_iota = lax.broadcasted_iota(jnp.int32, (N_KEYS, LANES), 0).astype(F32)
    sub_iota = lax.broadcasted_iota(jnp.int32, (SUBLANES, LANES), 0).astype(F32)

    def body(idx, carry):
        h = idx // n_lane_groups
        lanes = pl.ds(pl.multiple_of((idx % n_lane_groups) * LANES, LANES), LANES)
        s1 = s_scr[pl.ds(pl.multiple_of(h * 2 * N_KEYS, N_KEYS), N_KEYS), lanes]
        s2 = s_scr[pl.ds(pl.multiple_of(h * 2 * N_KEYS + N_KEYS, N_KEYS), N_KEYS), lanes]

        def emit(rank1, e1, rank2, e2, counts, z):
            lrow = jnp.zeros((N_KEYS, LANES), F32)
            for a in range(TOPK):
                lrow = jnp.where(rank1 == float(a), counts[a], lrow)
            a1_ref[pl.ds(h, 1), :, lanes] = e1[None]
            lrow_ref[pl.ds(h, 1), :, lanes] = lrow[None]
            r2_ref[pl.ds(h, 1), :, lanes] = rank2.astype(BF16)[None]
            a2_ref[pl.ds(h, 1), :, lanes] = (e2 / z).astype(BF16)[None]

        vals1, rank1, e1, _, _, ok1 = _top16_distinct(s1, sub_iota)
        _, rank2, e2, v2_lo, v2_hi, ok2 = _top16_distinct(s2, sub_iota)
        counts, z = _select16(vals1, v2_lo, v2_hi, sub_iota, tie_break=False)
        total = counts[0]
        for cnt in counts[1:]:
            total = total + cnt
        ok3 = jnp.max(jnp.abs(total - float(TOPK))) == 0.0
        emit(rank1, e1, rank2, e2, counts, z)

        @pl.when(jnp.logical_not(ok1 & ok2 & ok3))
        def _():
            vals1, rank1, e1, _, _ = _top16(s1, key_iota, sub_iota)
            _, rank2, e2, v2_lo, v2_hi = _top16(s2, key_iota, sub_iota)
            counts, z = _select16(vals1, v2_lo, v2_hi, sub_iota)
            emit(rank1, e1, rank2, e2, counts, z)

        return carry

    lax.fori_loop(0, PEER_HEADS * n_lane_groups, body, 0)


def _route(hnt, wq_t, keys, tb=256):
    D, T = hnt.shape
    dq = wq_t.shape[0]
    spec = pl.BlockSpec((PEER_HEADS, N_KEYS, tb), lambda i: (0, 0, i))
    shape = jax.ShapeDtypeStruct((PEER_HEADS, N_KEYS, T), F32)
    shape_b = jax.ShapeDtypeStruct((PEER_HEADS, N_KEYS, T), BF16)
    return pl.pallas_call(
        functools.partial(_route_kernel, tb=tb),
        grid=(T // tb,),
        in_specs=[
            pl.BlockSpec((D, tb), lambda i: (0, i)),
            pl.BlockSpec((dq, D), lambda i: (0, 0)),
            pl.BlockSpec((2 * PEER_HEADS, N_KEYS, N_KEYS), lambda i: (0, 0, 0)),
        ],
        out_specs=[spec, spec, spec, spec],
        out_shape=[shape, shape, shape_b, shape_b],
        scratch_shapes=[pltpu.VMEM((dq, tb), BF16), pltpu.VMEM((dq, tb), F32)],
        compiler_params=_cparams(("arbitrary",)),
        name="peer_route",
    )(hnt, wq_t, keys)


def _peer_kernel(hnt_ref, u_ref, vt_ref, a1_ref, lrow_ref, r2_ref, a2_ref, h_ref,
                 y_ref, acc_scr, g_scr, *, et, n_split):
    j = pl.program_id(1)
    tb = hnt_ref.shape[1]
    part = et // n_split
    pack = 2 * SUBLANES

    @pl.when(j == 0)
    def _():
        acc_scr[...] = jnp.zeros(acc_scr.shape, F32)

    def gate_rows(ht, s):
        for r in range(part // N_KEYS):
            i1 = s * (part // N_KEYS) + r
            lb = [jnp.broadcast_to(lrow_ref[h, i1:i1 + 1, :], (pack, tb)).astype(BF16)
                  for h in range(PEER_HEADS)]
            ab = [jnp.broadcast_to(a1_ref[h, i1:i1 + 1, :], (pack, tb)).astype(BF16)
                  for h in range(PEER_HEADS)]
            for c in range(N_KEYS // pack):
                i2 = slice(c * pack, (c + 1) * pack)
                rows = slice(r * N_KEYS + c * pack, r * N_KEYS + (c + 1) * pack)
                w = None
                for h in range(PEER_HEADS):
                    wh = jnp.where(r2_ref[h, i2, :] < lb[h], ab[h] * a2_ref[h, i2, :],
                                   jnp.zeros((), BF16))
                    w = wh if w is None else w + wh
                x = ht[rows, :]
                gelu = 0.5 * x * (1.0 + lax.erf(x * (2.0 ** -0.5)))
                g_scr[s, rows, :] = w * gelu.astype(BF16)

    hnt = hnt_ref[...]
    hts = [jnp.dot(u_ref[s * part:(s + 1) * part, :], hnt, preferred_element_type=F32)
           for s in range(n_split)]
    for s in range(n_split):
        gate_rows(hts[s], s)
        acc_scr[...] += jnp.dot(vt_ref[0, :, s * part:(s + 1) * part], g_scr[s],
                                preferred_element_type=F32)

    @pl.when(j == pl.num_programs(1) - 1)
    def _():
        y_ref[...] = h_ref[...] + acc_scr[...].T


def _peer(hnt, u, vt, a1, lrow, r2, a2, h, tb=512, et=1024, n_split=2):
    D, T = hnt.shape
    E = u.shape[0]
    rows_per_step = et // N_KEYS
    sel_i1 = pl.BlockSpec((PEER_HEADS, rows_per_step, tb), lambda i, j: (0, j, i))
    sel_i2 = pl.BlockSpec((PEER_HEADS, N_KEYS, tb), lambda i, j: (0, 0, i))
    return pl.pallas_call(
        functools.partial(_peer_kernel, et=et, n_split=n_split),
        grid=(T // tb, E // et),
        in_specs=[
            pl.BlockSpec((D, tb), lambda i, j: (0, i)),
            pl.BlockSpec((et, D), lambda i, j: (j, 0)),
            pl.BlockSpec((1, D, et), lambda i, j: (j, 0, 0)),
            sel_i1, sel_i1, sel_i2, sel_i2,
            pl.BlockSpec((tb, D), lambda i, j: (i, 0)),
        ],
        out_specs=pl.BlockSpec((tb, D), lambda i, j: (i, 0)),
        out_shape=jax.ShapeDtypeStruct((T, D), F32),
        scratch_shapes=[pltpu.VMEM((D, tb), F32),
                        pltpu.VMEM((n_split, et // n_split, tb), BF16)],
        compiler_params=_cparams(("arbitrary", "arbitrary")),
        name="peer_dense",
    )(hnt, u, vt, a1, lrow, r2, a2, h)


def _layer(h2, n_batch, seq, attn_g, w_in, b_forget, q_g, k_g, w_out, ffn_g,
           w_pq, sub_keys, u, v):
    T, D = h2.shape
    n_main = 7 * D_GROUP
    w_main = w_in[:, :n_main].astype(BF16)
    w_f = w_in[:, n_main:].T.astype(BF16)
    proj, gate, flog_t = _inproj(h2, attn_g.reshape(1, D), w_main, w_f,
                                 q_g.reshape(1, HEAD_DIM), k_g.reshape(1, HEAD_DIM))
    c = _forget_cumsum(flog_t, b_forget, seq)
    fox = _fox_attention(proj, gate, c.reshape(n_batch * N_HEADS, seq, 1),
                         c.reshape(n_batch * N_HEADS, 1, seq), n_batch, seq)
    sb = _sb_attention(proj, n_batch, seq)
    w_out_b = w_out.astype(BF16)
    h_mid, hnt = _outproj(fox, sb, h2, w_out_b[:D_GROUP], w_out_b[D_GROUP:],
                          ffn_g.reshape(1, D))
    keys = sub_keys.reshape(2 * PEER_HEADS, N_KEYS, -1).astype(BF16)
    a1, lrow, r2, a2 = _route(hnt, w_pq.T.astype(BF16), keys)
    vt = v.astype(BF16).reshape(-1, PEER_ET, D).transpose(0, 2, 1)
    return _peer(hnt, u.astype(BF16), vt, a1, lrow, r2, a2, h_mid, et=PEER_ET)


def kernel(x, attn_norm_g, w_in, b_forget, q_norm_g, k_norm_g, w_out, ffn_norm_g,
           w_peer_q, peer_sub_keys, peer_u, peer_v):
    n_batch, seq, d_model = x.shape
    h = x.reshape(n_batch * seq, d_model)
    for l in range(attn_norm_g.shape[0]):
        h = _layer(h, n_batch, seq, attn_norm_g[l], w_in[l], b_forget[l], q_norm_g[l],
                   k_norm_g[l], w_out[l], ffn_norm_g[l], w_peer_q[l], peer_sub_keys[l],
                   peer_u[l], peer_v[l])
    return h.reshape(n_batch, seq, d_model)
```

```python
import functools

import jax
import jax.numpy as jnp
from jax import lax
from jax.experimental import pallas as pl
from jax.experimental.pallas import tpu as pltpu

F32 = jnp.float32
BF16 = jnp.bfloat16
EPS = 1e-6
HEAD_DIM = 128
N_HEADS = 8
D_GROUP = N_HEADS * HEAD_DIM
N_KEYS = 128
PEER_HEADS = 8
TOPK = 16
LANES = 128
SUBLANES = 8
NEG_INF = float("-inf")
NOT_SELECTED = 99.0
SB_UNDERFLOW_LOG = -105.0
VMEM_LIMIT = 56 * 1024 * 1024
PEER_ET = 1024


def _cparams(sem):
    return pltpu.CompilerParams(dimension_semantics=sem, vmem_limit_bytes=VMEM_LIMIT)


def _log_sigmoid(z):
    return jnp.minimum(z, 0.0) - jnp.log1p(jnp.exp(-jnp.abs(z)))


def _inproj_kernel(x_ref, g_ref, w_ref, wf_ref, qg_ref, kg_ref,
                   out_ref, gate_ref, flog_ref, xn_scr):
    j = pl.program_id(1)

    @pl.when(j == 0)
    def _():
        x = x_ref[...]
        r = lax.rsqrt(jnp.mean(x * x, axis=-1, keepdims=True) + EPS)
        xn = ((x * r) * g_ref[...]).astype(BF16)
        xn_scr[...] = xn
        flog_ref[...] = lax.dot_general(wf_ref[...], xn, (((1,), (1,)), ((), ())),
                                        preferred_element_type=F32)

    p = jnp.dot(xn_scr[...], w_ref[...], preferred_element_type=F32)

    def head_norm(g):
        for h in range(N_HEADS):
            ph = p[:, h * HEAD_DIM:(h + 1) * HEAD_DIM]
            r = lax.rsqrt(jnp.mean(ph * ph, axis=-1, keepdims=True) + EPS)
            out_ref[0, :, h * HEAD_DIM:(h + 1) * HEAD_DIM] = ((ph * r) * g).astype(BF16)

    @pl.when(j == 0)
    def _():
        head_norm(qg_ref[...])

    @pl.when(j == 1)
    def _():
        head_norm(kg_ref[...])

    @pl.when(j == 3)
    def _():
        gate_ref[...] = jax.nn.sigmoid(p)
        out_ref[0] = p.astype(BF16)

    @pl.when((j == 2) | (j >= 4))
    def _():
        out_ref[0] = p.astype(BF16)


def _inproj(x2, g, w_main, w_f, qg, kg, tm=512):
    T, D = x2.shape
    n_groups = w_main.shape[1] // D_GROUP
    return pl.pallas_call(
        _inproj_kernel,
        grid=(T // tm, n_groups),
        in_specs=[
            pl.BlockSpec((tm, D), lambda i, j: (i, 0)),
            pl.BlockSpec((1, D), lambda i, j: (0, 0)),
            pl.BlockSpec((D, D_GROUP), lambda i, j: (0, j)),
            pl.BlockSpec((N_HEADS, D), lambda i, j: (0, 0)),
            pl.BlockSpec((1, HEAD_DIM), lambda i, j: (0, 0)),
            pl.BlockSpec((1, HEAD_DIM), lambda i, j: (0, 0)),
        ],
        out_specs=[
            pl.BlockSpec((1, tm, D_GROUP), lambda i, j: (j, i, 0)),
            pl.BlockSpec((tm, D_GROUP), lambda i, j: (i, 0)),
            pl.BlockSpec((N_HEADS, tm), lambda i, j: (0, i)),
        ],
        out_shape=[
            jax.ShapeDtypeStruct((n_groups, T, D_GROUP), BF16),
            jax.ShapeDtypeStruct((T, D_GROUP), F32),
            jax.ShapeDtypeStruct((N_HEADS, T), F32),
        ],
        scratch_shapes=[pltpu.VMEM((tm, D), BF16)],
        compiler_params=_cparams(("arbitrary", "arbitrary")),
        name="inproj",
    )(x2, g, w_main, w_f, qg, kg)


def _cumsum_kernel(flog_ref, b_ref, c_ref, *, seq):
    n_batch = flog_ref.shape[1] // seq
    lane = lax.broadcasted_iota(jnp.int32, (N_HEADS, seq), 1)
    for b in range(n_batch):
        x = _log_sigmoid(flog_ref[:, b * seq:(b + 1) * seq] + b_ref[...])
        sh = 1
        while sh < seq:
            x = x + jnp.where(lane >= sh, pltpu.roll(x, sh, axis=1), 0.0)
            sh *= 2
        c_ref[b * N_HEADS:(b + 1) * N_HEADS, :] = x


def _forget_cumsum(flog_t, b_forget, seq):
    T = flog_t.shape[1]
    n_batch = T // seq
    return pl.pallas_call(
        functools.partial(_cumsum_kernel, seq=seq),
        out_shape=jax.ShapeDtypeStruct((n_batch * N_HEADS, seq), F32),
        compiler_params=pltpu.CompilerParams(vmem_limit_bytes=VMEM_LIMIT),
        name="forget_cumsum",
    )(flog_t, b_forget.reshape(N_HEADS, 1))


def _fox_kernel(q_ref, k_ref, v_ref, gate_ref, cq_ref, ck_ref, o_ref,
                m_scr, acc_scr, *, tq, n_par, scale):
    qi = pl.program_id(1)
    m_scr[...] = jnp.full(m_scr.shape, NEG_INF, F32)
    acc_scr[...] = jnp.zeros(acc_scr.shape, F32)
    ones = jnp.ones((tq, HEAD_DIM), BF16)
    heads = [slice(g * HEAD_DIM, (g + 1) * HEAD_DIM) for g in range(n_par)]

    def block(kb, diagonal):
        off = pl.multiple_of(kb * tq, tq)
        for g in range(n_par):
            k = k_ref[0, pl.ds(off, tq), heads[g]]
            v = v_ref[0, pl.ds(off, tq), heads[g]]
            ck = ck_ref[g, :, pl.ds(off, tq)]
            s = lax.dot_general(q_ref[0, :, heads[g]], k, (((1,), (1,)), ((), ())),
                                preferred_element_type=F32)
            s = s * scale + cq_ref[g] - ck
            if diagonal:
                row = lax.broadcasted_iota(jnp.int32, (tq, tq), 0)
                col = lax.broadcasted_iota(jnp.int32, (tq, tq), 1)
                s = jnp.where(col <= row, s, NEG_INF)
            m = m_scr[g]
            m_new = jnp.maximum(m, jnp.max(s, axis=1, keepdims=True))
            alpha = jnp.exp(m - m_new)
            p = jnp.exp(s - m_new)
            pv = jnp.dot(p.astype(BF16), jnp.concatenate([v, ones], axis=1),
                         preferred_element_type=F32)
            acc_scr[g] = acc_scr[g] * alpha + pv
            m_scr[g] = m_new

    def body(kb, carry):
        block(kb, False)
        return carry

    block(qi, True)
    lax.fori_loop(0, qi, body, 0)
    for g in range(n_par):
        acc = acc_scr[g]
        o_ref[:, heads[g]] = ((acc[:, :HEAD_DIM] / acc[:, HEAD_DIM:])
                              * gate_ref[:, heads[g]]).astype(BF16)


def _fox_attention(proj, gate, c_col, c_row, n_batch, seq, tq=512, n_par=4):
    T = proj.shape[1]
    nq = seq // tq
    n_grp = N_HEADS // n_par
    w = n_par * HEAD_DIM
    return pl.pallas_call(
        functools.partial(_fox_kernel, tq=tq, n_par=n_par, scale=HEAD_DIM ** -0.5),
        grid=(n_batch * n_grp, nq),
        in_specs=[
            pl.BlockSpec((1, tq, w), lambda bg, i: (0, (bg // n_grp) * nq + i, bg % n_grp)),
            pl.BlockSpec((1, seq, w), lambda bg, i: (1, bg // n_grp, bg % n_grp)),
            pl.BlockSpec((1, seq, w), lambda bg, i: (2, bg // n_grp, bg % n_grp)),
            pl.BlockSpec((tq, w), lambda bg, i: ((bg // n_grp) * nq + i, bg % n_grp)),
            pl.BlockSpec((n_par, tq, 1), lambda bg, i: (bg, i, 0)),
            pl.BlockSpec((n_par, 1, seq), lambda bg, i: (bg, 0, 0)),
        ],
        out_specs=pl.BlockSpec((tq, w), lambda bg, i: ((bg // n_grp) * nq + i, bg % n_grp)),
        out_shape=jax.ShapeDtypeStruct((T, D_GROUP), BF16),
        scratch_shapes=[pltpu.VMEM((n_par, tq, 1), F32),
                        pltpu.VMEM((n_par, tq, 2 * HEAD_DIM), F32)],
        compiler_params=_cparams(("arbitrary", "arbitrary")),
        name="fox_attention",
    )(proj, proj, proj, gate, c_col, c_row)


def _sb_kernel(q_ref, k_ref, v_ref, uo_ref, o_ref, acc_scr, carry_scr, *, tq, n_par, scale):
    qi = pl.program_id(1)
    heads = [slice(p * HEAD_DIM, (p + 1) * HEAD_DIM) for p in range(n_par)]

    def group(p, g, valid):
        off = pl.multiple_of(g * tq, tq)
        k = k_ref[0, pl.ds(off, tq), heads[p]]
        v = v_ref[0, pl.ds(off, tq), heads[p]]
        z = lax.dot_general(q_ref[0, :, heads[p]], k, (((1,), (1,)), ((), ())),
                            preferred_element_type=F32) * scale
        lp = jnp.log(1.0 + jnp.exp(-jnp.abs(z)))
        t = jnp.minimum(z, 0.0)
        log_beta = t - lp
        log_1mb = (t - z) - lp
        if valid is not None:
            log_1mb = jnp.where(valid, log_1mb, 0.0)
        hi = log_1mb.astype(BF16)
        lo = (log_1mb - hi.astype(F32)).astype(BF16)
        cs = jnp.dot(jnp.concatenate([hi, lo], axis=1), uo_ref[...],
                     preferred_element_type=F32)
        return log_beta, cs[:, :tq], cs[:, tq:], v

    row = lax.broadcasted_iota(jnp.int32, (tq, tq), 0)
    col = lax.broadcasted_iota(jnp.int32, (tq, tq), 1)
    causal = col < row
    has_prev = qi > 0
    for p in range(n_par):
        lb0, suf0, tot0, v0 = group(p, qi, causal)
        lb1, suf1, tot1, v1 = group(p, jnp.maximum(qi - 1, 0), has_prev)
        a0 = jnp.where(causal, jnp.exp(lb0 + suf0), 0.0)
        a1 = jnp.where(has_prev, jnp.exp(lb1 + suf1 + tot0), 0.0)
        acc_scr[p] = (jnp.dot(a0.astype(BF16), v0, preferred_element_type=F32)
                      + jnp.dot(a1.astype(BF16), v1, preferred_element_type=F32))
        carry_scr[p] = tot0 + tot1

    def live():
        return jnp.max(carry_scr[...]) > SB_UNDERFLOW_LOG

    def body(state):
        g, _ = state
        for p in range(n_par):
            lb, suf, tot, v = group(p, g, None)
            carry = carry_scr[p]
            a = jnp.exp(lb + suf + carry)
            acc_scr[p] += jnp.dot(a.astype(BF16), v, preferred_element_type=F32)
            carry_scr[p] = carry + tot
        return g - 1, live()

    lax.while_loop(lambda state: (state[0] >= 0) & state[1], body, (qi - 2, live()))
    for p in range(n_par):
        o_ref[:, heads[p]] = acc_scr[p].astype(BF16)


def _suffix_matrix(n):
    j = jnp.arange(2 * n)[:, None] % n
    s = jnp.arange(n)[None, :]
    tri = (j > s).astype(BF16)
    return jnp.concatenate([tri, jnp.ones((2 * n, n), BF16)], axis=1)


def _sb_attention(proj, n_batch, seq, tq=256, n_par=2):
    T = proj.shape[1]
    nq = seq // tq
    n_grp = N_HEADS // n_par
    w = n_par * HEAD_DIM
    return pl.pallas_call(
        functools.partial(_sb_kernel, tq=tq, n_par=n_par, scale=HEAD_DIM ** -0.5),
        grid=(n_batch * n_grp, nq),
        in_specs=[
            pl.BlockSpec((1, tq, w), lambda bg, i: (4, (bg // n_grp) * nq + i, bg % n_grp)),
            pl.BlockSpec((1, seq, w), lambda bg, i: (5, bg // n_grp, bg % n_grp)),
            pl.BlockSpec((1, seq, w), lambda bg, i: (6, bg // n_grp, bg % n_grp)),
            pl.BlockSpec((2 * tq, 2 * tq), lambda bg, i: (0, 0)),
        ],
        out_specs=pl.BlockSpec((tq, w), lambda bg, i: ((bg // n_grp) * nq + i, bg % n_grp)),
        out_shape=jax.ShapeDtypeStruct((T, D_GROUP), BF16),
        scratch_shapes=[pltpu.VMEM((n_par, tq, HEAD_DIM), F32),
                        pltpu.VMEM((n_par, tq, tq), F32)],
        compiler_params=_cparams(("arbitrary", "arbitrary")),
        name="sb_attention",
    )(proj, proj, proj, _suffix_matrix(tq))


def _outproj_kernel(fox_ref, sb_ref, x_ref, wo1_ref, wo2_ref, g_ref, h_ref, hnt_ref):
    h = (x_ref[...]
         + jnp.dot(fox_ref[...], wo1_ref[...], preferred_element_type=F32)
         + jnp.dot(sb_ref[...], wo2_ref[...], preferred_element_type=F32))
    h_ref[...] = h
    r = lax.rsqrt(jnp.mean(h * h, axis=-1, keepdims=True) + EPS)
    hn = (h * r) * g_ref[...]
    hnt_ref[...] = hn.T.astype(BF16)


def _outproj(fox, sb, x2, wo1, wo2, g, tm=512):
    T, D = x2.shape
    return pl.pallas_call(
        _outproj_kernel,
        grid=(T // tm,),
        in_specs=[
            pl.BlockSpec((tm, D_GROUP), lambda i: (i, 0)),
            pl.BlockSpec((tm, D_GROUP), lambda i: (i, 0)),
            pl.BlockSpec((tm, D), lambda i: (i, 0)),
            pl.BlockSpec((D_GROUP, D), lambda i: (0, 0)),
            pl.BlockSpec((D_GROUP, D), lambda i: (0, 0)),
            pl.BlockSpec((1, D), lambda i: (0, 0)),
        ],
        out_specs=[
            pl.BlockSpec((tm, D), lambda i: (i, 0)),
            pl.BlockSpec((D, tm), lambda i: (0, i)),
        ],
        out_shape=[jax.ShapeDtypeStruct((T, D), F32), jax.ShapeDtypeStruct((D, T), BF16)],
        compiler_params=_cparams(("arbitrary",)),
        name="outproj",
    )(fox, sb, x2, wo1, wo2, g)


def _max0(x):
    return jnp.max(x, axis=0, keepdims=True)


def _min0(x):
    return jnp.min(x, axis=0, keepdims=True)


def _top16(s, key_iota, sub_iota):
    rank = jnp.full(s.shape, NOT_SELECTED, F32)
    ew = jnp.zeros(s.shape, F32)
    v_lo = jnp.zeros((SUBLANES, LANES), F32)
    v_hi = jnp.zeros((SUBLANES, LANES), F32)
    vals = []
    for it in range(TOPK):
        m = _max0(s)
        first = _min0(jnp.where(s == m, key_iota, float(N_KEYS)))
        hit = key_iota == first
        s = jnp.where(hit, NEG_INF, s)
        rank = jnp.where(hit, float(it), rank)
        ew = jnp.where(hit, jnp.exp(m - vals[0]) if vals else 1.0, ew)
        if it < SUBLANES:
            v_lo = jnp.where(sub_iota == float(it), m, v_lo)
        else:
            v_hi = jnp.where(sub_iota == float(it - SUBLANES), m, v_hi)
        vals.append(m)
    return vals, rank, ew, v_lo, v_hi


def _top16_distinct(s, sub_iota):
    s0 = s
    rank = jnp.full(s.shape, NOT_SELECTED, F32)
    v_lo = jnp.zeros((SUBLANES, LANES), F32)
    v_hi = jnp.zeros((SUBLANES, LANES), F32)
    vals = []
    for it in range(TOPK):
        m = _max0(s)
        hit = s == m
        s = jnp.where(hit, NEG_INF, s)
        rank = jnp.where(hit, float(it), rank)
        if it < SUBLANES:
            v_lo = jnp.where(sub_iota == float(it), m, v_lo)
        else:
            v_hi = jnp.where(sub_iota == float(it - SUBLANES), m, v_hi)
        vals.append(m)
    taken = s == NEG_INF
    ew = jnp.where(taken, jnp.exp(s0 - vals[0]), 0.0)
    n_taken = jnp.sum(jnp.where(taken, 1.0, 0.0), axis=0, keepdims=True)
    ok = jnp.max(jnp.abs(n_taken - float(TOPK))) == 0.0
    return vals, rank, ew, v_lo, v_hi, ok


def _select16(vals1, v2_lo, v2_hi, sub_iota, tie_break=True):
    slabs, pos, n_invalid = [], [], []
    for a in range(TOPK):
        nb = min(TOPK // (a + 1), SUBLANES)
        c = vals1[a] + v2_lo
        if nb < SUBLANES:
            c = jnp.where(sub_iota < float(nb), c, NEG_INF)
        slabs.append(c)
        pos.append(sub_iota + float(TOPK * a))
        n_invalid.append(float(SUBLANES - nb))
    slabs.append(vals1[0] + v2_hi)
    pos.append(sub_iota + float(SUBLANES))
    n_invalid.append(0.0)

    def tree(op, xs):
        xs = list(xs)
        while len(xs) > 1:
            xs = [op(xs[i], xs[i + 1]) if i + 1 < len(xs) else xs[i]
                  for i in range(0, len(xs), 2)]
        return xs[0]

    z = None
    c_max = None
    for it in range(TOPK):
        m = _max0(tree(jnp.maximum, slabs))
        if tie_break:
            first = _min0(tree(jnp.minimum,
                               [jnp.where(c == m, p, 999.0) for c, p in zip(slabs, pos)]))
            slabs = [jnp.where(p == first, NEG_INF, c) for c, p in zip(slabs, pos)]
        else:
            slabs = [jnp.where(c == m, NEG_INF, c) for c in slabs]
        if it == 0:
            c_max = m
            z = jnp.ones_like(m)
        else:
            z = z + jnp.exp(m - c_max)
    counts = []
    for i, c in enumerate(slabs):
        taken = jnp.sum(jnp.where(c == NEG_INF, 1.0, 0.0), axis=0, keepdims=True)
        counts.append(taken - n_invalid[i])
    counts[0] = counts[0] + counts.pop()
    return counts, z


def _route_kernel(hnt_ref, wq_ref, keys_ref, a1_ref, lrow_ref, r2_ref, a2_ref,
                  q_scr, s_scr, *, tb):
    qt = jnp.dot(wq_ref[...], hnt_ref[...], preferred_element_type=F32)
    q_scr[...] = qt.astype(BF16)
    for hc in range(2 * PEER_HEADS):
        rows = slice(hc * N_KEYS, (hc + 1) * N_KEYS)
        s_scr[rows, :] = jnp.dot(keys_ref[hc], q_scr[rows, :], preferred_element_type=F32)

    n_lane_groups = tb // LANES
    key_iota = lax.broadcasted_iota(jnp.int32, (N_KEYS, LANES), 0).astype(F32)
    sub_iota = lax.broadcasted_iota(jnp.int32, (SUBLANES, LANES), 0).astype(F32)

    def body(idx, carry):
        h = idx // n_lane_groups
        lanes = pl.ds(pl.multiple_of((idx % n_lane_groups) * LANES, LANES), LANES)
        s1 = s_scr[pl.ds(pl.multiple_of(h * 2 * N_KEYS, N_KEYS), N_KEYS), lanes]
        s2 = s_scr[pl.ds(pl.multiple_of(h * 2 * N_KEYS + N_KEYS, N_KEYS), N_KEYS), lanes]

        def emit(rank1, e1, rank2, e2, counts, z):
            lrow = jnp.zeros((N_KEYS, LANES), F32)
            for a in range(TOPK):
                lrow = jnp.where(rank1 == float(a), counts[a], lrow)
            a1_ref[pl.ds(h, 1), :, lanes] = e1[None]
            lrow_ref[pl.ds(h, 1), :, lanes] = lrow[None]
            r2_ref[pl.ds(h, 1), :, lanes] = rank2.astype(BF16)[None]
            a2_ref[pl.ds(h, 1), :, lanes] = (e2 / z).astype(BF16)[None]

        vals1, rank1, e1, _, _, ok1 = _top16_distinct(s1, sub_iota)
        _, rank2, e2, v2_lo, v2_hi, ok2 = _top16_distinct(s2, sub_iota)
        counts, z = _select16(vals1, v2_lo, v2_hi, sub_iota, tie_break=False)
        total = counts[0]
        for cnt in counts[1:]:
            total = total + cnt
        ok3 = jnp.max(jnp.abs(total - float(TOPK))) == 0.0
        emit(rank1, e1, rank2, e2, counts, z)

        @pl.when(jnp.logical_not(ok1 & ok2 & ok3))
        def _():
            vals1, rank1, e1, _, _ = _top16(s1, key_iota, sub_iota)
            _, rank2, e2, v2_lo, v2_hi = _top16(s2, key_iota, sub_iota)
            counts, z = _select16(vals1, v2_lo, v2_hi, sub_iota)
            emit(rank1, e1, rank2, e2, counts, z)

        return carry

    lax.fori_loop(0, PEER_HEADS * n_lane_groups, body, 0)


def _route(hnt, wq_t, keys, tb=256):
    D, T = hnt.shape
    dq = wq_t.shape[0]
    spec = pl.BlockSpec((PEER_HEADS, N_KEYS, tb), lambda i: (0, 0, i))
    shape = jax.ShapeDtypeStruct((PEER_HEADS, N_KEYS, T), F32)
    shape_b = jax.ShapeDtypeStruct((PEER_HEADS, N_KEYS, T), BF16)
    return pl.pallas_call(
        functools.partial(_route_kernel, tb=tb),
        grid=(T // tb,),
        in_specs=[
            pl.BlockSpec((D, tb), lambda i: (0, i)),
            pl.BlockSpec((dq, D), lambda i: (0, 0)),
            pl.BlockSpec((2 * PEER_HEADS, N_KEYS, N_KEYS), lambda i: (0, 0, 0)),
        ],
        out_specs=[spec, spec, spec, spec],
        out_shape=[shape, shape, shape_b, shape_b],
        scratch_shapes=[pltpu.VMEM((dq, tb), BF16), pltpu.VMEM((dq, tb), F32)],
        compiler_params=_cparams(("arbitrary",)),
        name="peer_route",
    )(hnt, wq_t, keys)


def _peer_kernel(hnt_ref, u_ref, vt_ref, a1_ref, lrow_ref, r2_ref, a2_ref, h_ref,
                 y_ref, acc_scr, g_scr, *, et, n_split):
    j = pl.program_id(1)
    tb = hnt_ref.shape[1]
    part = et // n_split
    pack = 2 * SUBLANES

    @pl.when(j == 0)
    def _():
        acc_scr[...] = jnp.zeros(acc_scr.shape, F32)

    def gate_rows(ht, s):
        for r in range(part // N_KEYS):
            i1 = s * (part // N_KEYS) + r
            lb = [jnp.broadcast_to(lrow_ref[h, i1:i1 + 1, :], (pack, tb)).astype(BF16)
                  for h in range(PEER_HEADS)]
            ab = [jnp.broadcast_to(a1_ref[h, i1:i1 + 1, :], (pack, tb)).astype(BF16)
                  for h in range(PEER_HEADS)]
            for c in range(N_KEYS // pack):
                i2 = slice(c * pack, (c + 1) * pack)
                rows = slice(r * N_KEYS + c * pack, r * N_KEYS + (c + 1) * pack)
                w = None
                for h in range(PEER_HEADS):
                    wh = jnp.where(r2_ref[h, i2, :] < lb[h], ab[h] * a2_ref[h, i2, :],
                                   jnp.zeros((), BF16))
                    w = wh if w is None else w + wh
                x = ht[rows, :]
                gelu = 0.5 * x * (1.0 + lax.erf(x * (2.0 ** -0.5)))
                g_scr[s, rows, :] = w * gelu.astype(BF16)

    hnt = hnt_ref[...]
    hts = [jnp.dot(u_ref[s * part:(s + 1) * part, :], hnt, preferred_element_type=F32)
           for s in range(n_split)]
    for s in range(n_split):
        gate_rows(hts[s], s)
        acc_scr[...] += jnp.dot(vt_ref[0, :, s * part:(s + 1) * part], g_scr[s],
                                preferred_element_type=F32)

    @pl.when(j == pl.num_programs(1) - 1)
    def _():
        y_ref[...] = h_ref[...] + acc_scr[...].T


def _peer(hnt, u, vt, a1, lrow, r2, a2, h, tb=512, et=1024, n_split=4):
    D, T = hnt.shape
    E = u.shape[0]
    rows_per_step = et // N_KEYS
    sel_i1 = pl.BlockSpec((PEER_HEADS, rows_per_step, tb), lambda i, j: (0, j, i))
    sel_i2 = pl.BlockSpec((PEER_HEADS, N_KEYS, tb), lambda i, j: (0, 0, i))
    return pl.pallas_call(
        functools.partial(_peer_kernel, et=et, n_split=n_split),
        grid=(T // tb, E // et),
        in_specs=[
            pl.BlockSpec((D, tb), lambda i, j: (0, i)),
            pl.BlockSpec((et, D), lambda i, j: (j, 0)),
            pl.BlockSpec((1, D, et), lambda i, j: (j, 0, 0)),
            sel_i1, sel_i1, sel_i2, sel_i2,
            pl.BlockSpec((tb, D), lambda i, j: (i, 0)),
        ],
        out_specs=pl.BlockSpec((tb, D), lambda i, j: (i, 0)),
        out_shape=jax.ShapeDtypeStruct((T, D), F32),
        scratch_shapes=[pltpu.VMEM((D, tb), F32),
                        pltpu.VMEM((n_split, et // n_split, tb), BF16)],
        compiler_params=_cparams(("arbitrary", "arbitrary")),
        name="peer_dense",
    )(hnt, u, vt, a1, lrow, r2, a2, h)


def _layer(h2, n_batch, seq, attn_g, w_in, b_forget, q_g, k_g, w_out, ffn_g,
           w_pq, sub_keys, u, v):
    T, D = h2.shape
    n_main = 7 * D_GROUP
    w_main = w_in[:, :n_main].astype(BF16)
    w_f = w_in[:, n_main:].T.astype(BF16)
    proj, gate, flog_t = _inproj(h2, attn_g.reshape(1, D), w_main, w_f,
                                 q_g.reshape(1, HEAD_DIM), k_g.reshape(1, HEAD_DIM))
    c = _forget_cumsum(flog_t, b_forget, seq)
    fox = _fox_attention(proj, gate, c.reshape(n_batch * N_HEADS, seq, 1),
                         c.reshape(n_batch * N_HEADS, 1, seq), n_batch, seq)
    sb = _sb_attention(proj, n_batch, seq)
    w_out_b = w_out.astype(BF16)
    h_mid, hnt = _outproj(fox, sb, h2, w_out_b[:D_GROUP], w_out_b[D_GROUP:],
                          ffn_g.reshape(1, D))
    keys = sub_keys.reshape(2 * PEER_HEADS, N_KEYS, -1).astype(BF16)
    a1, lrow, r2, a2 = _route(hnt, w_pq.T.astype(BF16), keys)
    vt = v.astype(BF16).reshape(-1, PEER_ET, D).transpose(0, 2, 1)
    return _peer(hnt, u.astype(BF16), vt, a1, lrow, r2, a2, h_mid, et=PEER_ET)


def kernel(x, attn_norm_g, w_in, b_forget, q_norm_g, k_norm_g, w_out, ffn_norm_g,
           w_peer_q, peer_sub_keys, peer_u, peer_v):
    n_batch, seq, d_model = x.shape
    h = x.reshape(n_batch * seq, d_model)
    for l in range(attn_norm_g.shape[0]):
        h = _layer(h, n_batch, seq, attn_norm_g[l], w_in[l], b_forget[l], q_norm_g[l],
                   k_norm_g[l], w_out[l], ffn_norm_g[l], w_peer_q[l], peer_sub_keys[l],
                   peer_u[l], peer_v[l])
    return h.reshape(n_batch, seq, d_model)
```

```python
import functools

import jax
import jax.numpy as jnp
from jax import lax
from jax.experimental import pallas as pl
from jax.experimental.pallas import tpu as pltpu

F32 = jnp.float32
BF16 = jnp.bfloat16
EPS = 1e-6
HEAD_DIM = 128
N_HEADS = 8
D_GROUP = N_HEADS * HEAD_DIM
N_KEYS = 128
PEER_HEADS = 8
TOPK = 16
LANES = 128
SUBLANES = 8
NEG_INF = float("-inf")
NOT_SELECTED = 99.0
SB_UNDERFLOW_LOG = -105.0
VMEM_LIMIT = 56 * 1024 * 1024
PEER_ET = 1024


def _cparams(sem):
    return pltpu.CompilerParams(dimension_semantics=sem, vmem_limit_bytes=VMEM_LIMIT)


def _log_sigmoid(z):
    return jnp.minimum(z, 0.0) - jnp.log1p(jnp.exp(-jnp.abs(z)))


def _inproj_kernel(x_ref, g_ref, w_ref, wf_ref, qg_ref, kg_ref,
                   out_ref, gate_ref, flog_ref, xn_scr):
    j = pl.program_id(1)

    @pl.when(j == 0)
    def _():
        x = x_ref[...]
        r = lax.rsqrt(jnp.mean(x * x, axis=-1, keepdims=True) + EPS)
        xn = ((x * r) * g_ref[...]).astype(BF16)
        xn_scr[...] = xn
        flog_ref[...] = lax.dot_general(wf_ref[...], xn, (((1,), (1,)), ((), ())),
                                        preferred_element_type=F32)

    p = jnp.dot(xn_scr[...], w_ref[...], preferred_element_type=F32)

    def head_norm(g):
        for h in range(N_HEADS):
            ph = p[:, h * HEAD_DIM:(h + 1) * HEAD_DIM]
            r = lax.rsqrt(jnp.mean(ph * ph, axis=-1, keepdims=True) + EPS)
            out_ref[0, :, h * HEAD_DIM:(h + 1) * HEAD_DIM] = ((ph * r) * g).astype(BF16)

    @pl.when(j == 0)
    def _():
        head_norm(qg_ref[...])

    @pl.when(j == 1)
    def _():
        head_norm(kg_ref[...])

    @pl.when(j == 3)
    def _():
        gate_ref[...] = jax.nn.sigmoid(p)
        out_ref[0] = p.astype(BF16)

    @pl.when((j == 2) | (j >= 4))
    def _():
        out_ref[0] = p.astype(BF16)


def _inproj(x2, g, w_main, w_f, qg, kg, tm=512):
    T, D = x2.shape
    n_groups = w_main.shape[1] // D_GROUP
    return pl.pallas_call(
        _inproj_kernel,
        grid=(T // tm, n_groups),
        in_specs=[
            pl.BlockSpec((tm, D), lambda i, j: (i, 0)),
            pl.BlockSpec((1, D), lambda i, j: (0, 0)),
            pl.BlockSpec((D, D_GROUP), lambda i, j: (0, j)),
            pl.BlockSpec((N_HEADS, D), lambda i, j: (0, 0)),
            pl.BlockSpec((1, HEAD_DIM), lambda i, j: (0, 0)),
            pl.BlockSpec((1, HEAD_DIM), lambda i, j: (0, 0)),
        ],
        out_specs=[
            pl.BlockSpec((1, tm, D_GROUP), lambda i, j: (j, i, 0)),
            pl.BlockSpec((tm, D_GROUP), lambda i, j: (i, 0)),
            pl.BlockSpec((N_HEADS, tm), lambda i, j: (0, i)),
        ],
        out_shape=[
            jax.ShapeDtypeStruct((n_groups, T, D_GROUP), BF16),
            jax.ShapeDtypeStruct((T, D_GROUP), F32),
            jax.ShapeDtypeStruct((N_HEADS, T), F32),
        ],
        scratch_shapes=[pltpu.VMEM((tm, D), BF16)],
        compiler_params=_cparams(("arbitrary", "arbitrary")),
        name="inproj",
    )(x2, g, w_main, w_f, qg, kg)


def _cumsum_kernel(flog_ref, b_ref, c_ref, *, seq):
    n_batch = flog_ref.shape[1] // seq
    lane = lax.broadcasted_iota(jnp.int32, (N_HEADS, seq), 1)
    for b in range(n_batch):
        x = _log_sigmoid(flog_ref[:, b * seq:(b + 1) * seq] + b_ref[...])
        sh = 1
        while sh < seq:
            x = x + jnp.where(lane >= sh, pltpu.roll(x, sh, axis=1), 0.0)
            sh *= 2
        c_ref[b * N_HEADS:(b + 1) * N_HEADS, :] = x


def _forget_cumsum(flog_t, b_forget, seq):
    T = flog_t.shape[1]
    n_batch = T // seq
    return pl.pallas_call(
        functools.partial(_cumsum_kernel, seq=seq),
        out_shape=jax.ShapeDtypeStruct((n_batch * N_HEADS, seq), F32),
        compiler_params=pltpu.CompilerParams(vmem_limit_bytes=VMEM_LIMIT),
        name="forget_cumsum",
    )(flog_t, b_forget.reshape(N_HEADS, 1))


def _fox_kernel(q_ref, k_ref, v_ref, gate_ref, cq_ref, ck_ref, o_ref,
                m_scr, acc_scr, *, tq, n_par, scale):
    qi = pl.program_id(1)
    m_scr[...] = jnp.full(m_scr.shape, NEG_INF, F32)
    acc_scr[...] = jnp.zeros(acc_scr.shape, F32)
    ones = jnp.ones((tq, HEAD_DIM), BF16)
    heads = [slice(g * HEAD_DIM, (g + 1) * HEAD_DIM) for g in range(n_par)]

    def block(kb, diagonal):
        off = pl.multiple_of(kb * tq, tq)
        for g in range(n_par):
            k = k_ref[0, pl.ds(off, tq), heads[g]]
            v = v_ref[0, pl.ds(off, tq), heads[g]]
            ck = ck_ref[g, :, pl.ds(off, tq)]
            s = lax.dot_general(q_ref[0, :, heads[g]], k, (((1,), (1,)), ((), ())),
                                preferred_element_type=F32)
            s = s * scale + cq_ref[g] - ck
            if diagonal:
                row = lax.broadcasted_iota(jnp.int32, (tq, tq), 0)
                col = lax.broadcasted_iota(jnp.int32, (tq, tq), 1)
                s = jnp.where(col <= row, s, NEG_INF)
            m = m_scr[g]
            m_new = jnp.maximum(m, jnp.max(s, axis=1, keepdims=True))
            alpha = jnp.exp(m - m_new)
            p = jnp.exp(s - m_new)
            pv = jnp.dot(p.astype(BF16), jnp.concatenate([v, ones], axis=1),
                         preferred_element_type=F32)
            acc_scr[g] = acc_scr[g] * alpha + pv
            m_scr[g] = m_new

    def body(kb, carry):
        block(kb, False)
        return carry

    block(qi, True)
    lax.fori_loop(0, qi, body, 0)
    for g in range(n_par):
        acc = acc_scr[g]
        o_ref[:, heads[g]] = ((acc[:, :HEAD_DIM] / acc[:, HEAD_DIM:])
                              * gate_ref[:, heads[g]]).astype(BF16)


def _fox_attention(proj, gate, c_col, c_row, n_batch, seq, tq=512, n_par=4):
    T = proj.shape[1]
    nq = seq // tq
    n_grp = N_HEADS // n_par
    w = n_par * HEAD_DIM
    return pl.pallas_call(
        functools.partial(_fox_kernel, tq=tq, n_par=n_par, scale=HEAD_DIM ** -0.5),
        grid=(n_batch * n_grp, nq),
        in_specs=[
            pl.BlockSpec((1, tq, w), lambda bg, i: (0, (bg // n_grp) * nq + i, bg % n_grp)),
            pl.BlockSpec((1, seq, w), lambda bg, i: (1, bg // n_grp, bg % n_grp)),
            pl.BlockSpec((1, seq, w), lambda bg, i: (2, bg // n_grp, bg % n_grp)),
            pl.BlockSpec((tq, w), lambda bg, i: ((bg // n_grp) * nq + i, bg % n_grp)),
            pl.BlockSpec((n_par, tq, 1), lambda bg, i: (bg, i, 0)),
            pl.BlockSpec((n_par, 1, seq), lambda bg, i: (bg, 0, 0)),
        ],
        out_specs=pl.BlockSpec((tq, w), lambda bg, i: ((bg // n_grp) * nq + i, bg % n_grp)),
        out_shape=jax.ShapeDtypeStruct((T, D_GROUP), BF16),
        scratch_shapes=[pltpu.VMEM((n_par, tq, 1), F32),
                        pltpu.VMEM((n_par, tq, 2 * HEAD_DIM), F32)],
        compiler_params=_cparams(("arbitrary", "arbitrary")),
        name="fox_attention",
    )(proj, proj, proj, gate, c_col, c_row)


def _sb_kernel(q_ref, k_ref, v_ref, uo_ref, o_ref, acc_scr, carry_scr, *, tq, n_par, scale):
    qi = pl.program_id(1)
    heads = [slice(p * HEAD_DIM, (p + 1) * HEAD_DIM) for p in range(n_par)]

    def group(p, g, valid):
        off = pl.multiple_of(g * tq, tq)
        k = k_ref[0, pl.ds(off, tq), heads[p]]
        v = v_ref[0, pl.ds(off, tq), heads[p]]
        z = lax.dot_general(q_ref[0, :, heads[p]], k, (((1,), (1,)), ((), ())),
                            preferred_element_type=F32) * scale
        lp = jnp.log(1.0 + jnp.exp(-jnp.abs(z)))
        t = jnp.minimum(z, 0.0)
        log_beta = t - lp
        log_1mb = (t - z) - lp
        if valid is not None:
            log_1mb = jnp.where(valid, log_1mb, 0.0)
        hi = log_1mb.astype(BF16)
        lo = (log_1mb - hi.astype(F32)).astype(BF16)
        cs = jnp.dot(jnp.concatenate([hi, lo], axis=1), uo_ref[...],
                     preferred_element_type=F32)
        return log_beta, cs[:, :tq], cs[:, tq:], v

    row = lax.broadcasted_iota(jnp.int32, (tq, tq), 0)
    col = lax.broadcasted_iota(jnp.int32, (tq, tq), 1)
    causal = col < row
    has_prev = qi > 0
    for p in range(n_par):
        lb0, suf0, tot0, v0 = group(p, qi, causal)
        lb1, suf1, tot1, v1 = group(p, jnp.maximum(qi - 1, 0), has_prev)
        a0 = jnp.where(causal, jnp.exp(lb0 + suf0), 0.0)
        a1 = jnp.where(has_prev, jnp.exp(lb1 + suf1 + tot0), 0.0)
        acc_scr[p] = (jnp.dot(a0.astype(BF16), v0, preferred_element_type=F32)
                      + jnp.dot(a1.astype(BF16), v1, preferred_element_type=F32))
        carry_scr[p] = tot0 + tot1

    def live():
        return jnp.max(carry_scr[...]) > SB_UNDERFLOW_LOG

    def body(state):
        g, _ = state
        for p in range(n_par):
            lb, suf, tot, v = group(p, g, None)
            carry = carry_scr[p]
            a = jnp.exp(lb + suf + carry)
            acc_scr[p] += jnp.dot(a.astype(BF16), v, preferred_element_type=F32)
            carry_scr[p] = carry + tot
        return g - 1, live()

    lax.while_loop(lambda state: (state[0] >= 0) & state[1], body, (qi - 2, live()))
    for p in range(n_par):
        o_ref[:, heads[p]] = acc_scr[p].astype(BF16)


def _suffix_matrix(n):
    j = jnp.arange(2 * n)[:, None] % n
    s = jnp.arange(n)[None, :]
    tri = (j > s).astype(BF16)
    return jnp.concatenate([tri, jnp.ones((2 * n, n), BF16)], axis=1)


def _sb_attention(proj, n_batch, seq, tq=256, n_par=2):
    T = proj.shape[1]
    nq = seq // tq
    n_grp = N_HEADS // n_par
    w = n_par * HEAD_DIM
    return pl.pallas_call(
        functools.partial(_sb_kernel, tq=tq, n_par=n_par, scale=HEAD_DIM ** -0.5),
        grid=(n_batch * n_grp, nq),
        in_specs=[
            pl.BlockSpec((1, tq, w), lambda bg, i: (4, (bg // n_grp) * nq + i, bg % n_grp)),
            pl.BlockSpec((1, seq, w), lambda bg, i: (5, bg // n_grp, bg % n_grp)),
            pl.BlockSpec((1, seq, w), lambda bg, i: (6, bg // n_grp, bg % n_grp)),
            pl.BlockSpec((2 * tq, 2 * tq), lambda bg, i: (0, 0)),
        ],
        out_specs=pl.BlockSpec((tq, w), lambda bg, i: ((bg // n_grp) * nq + i, bg % n_grp)),
        out_shape=jax.ShapeDtypeStruct((T, D_GROUP), BF16),
        scratch_shapes=[pltpu.VMEM((n_par, tq, HEAD_DIM), F32),
                        pltpu.VMEM((n_par, tq, tq), F32)],
        compiler_params=_cparams(("arbitrary", "arbitrary")),
        name="sb_attention",
    )(proj, proj, proj, _suffix_matrix(tq))


def _outproj_kernel(fox_ref, sb_ref, x_ref, wo1_ref, wo2_ref, g_ref, h_ref, hnt_ref):
    h = (x_ref[...]
         + jnp.dot(fox_ref[...], wo1_ref[...], preferred_element_type=F32)
         + jnp.dot(sb_ref[...], wo2_ref[...], preferred_element_type=F32))
    h_ref[...] = h
    r = lax.rsqrt(jnp.mean(h * h, axis=-1, keepdims=True) + EPS)
    hn = (h * r) * g_ref[...]
    hnt_ref[...] = hn.T.astype(BF16)


def _outproj(fox, sb, x2, wo, g, tm=512):
    T, D = x2.shape
    return pl.pallas_call(
        _outproj_kernel,
        grid=(T // tm,),
        in_specs=[
            pl.BlockSpec((tm, D_GROUP), lambda i: (i, 0)),
            pl.BlockSpec((tm, D_GROUP), lambda i: (i, 0)),
            pl.BlockSpec((tm, D), lambda i: (i, 0)),
            pl.BlockSpec((D_GROUP, D), lambda i: (0, 0)),
            pl.BlockSpec((D_GROUP, D), lambda i: (1, 0)),
            pl.BlockSpec((1, D), lambda i: (0, 0)),
        ],
        out_specs=[
            pl.BlockSpec((tm, D), lambda i: (i, 0)),
            pl.BlockSpec((D, tm), lambda i: (0, i)),
        ],
        out_shape=[jax.ShapeDtypeStruct((T, D), F32), jax.ShapeDtypeStruct((D, T), BF16)],
        compiler_params=_cparams(("arbitrary",)),
        name="outproj",
    )(fox, sb, x2, wo, wo, g)


def _max0(x):
    return jnp.max(x, axis=0, keepdims=True)


def _min0(x):
    return jnp.min(x, axis=0, keepdims=True)


def _top16(s, key_iota, sub_iota):
    rank = jnp.full(s.shape, NOT_SELECTED, F32)
    ew = jnp.zeros(s.shape, F32)
    v_lo = jnp.zeros((SUBLANES, LANES), F32)
    v_hi = jnp.zeros((SUBLANES, LANES), F32)
    vals = []
    for it in range(TOPK):
        m = _max0(s)
        first = _min0(jnp.where(s == m, key_iota, float(N_KEYS)))
        hit = key_iota == first
        s = jnp.where(hit, NEG_INF, s)
        rank = jnp.where(hit, float(it), rank)
        ew = jnp.where(hit, jnp.exp(m - vals[0]) if vals else 1.0, ew)
        if it < SUBLANES:
            v_lo = jnp.where(sub_iota == float(it), m, v_lo)
        else:
            v_hi = jnp.where(sub_iota == float(it - SUBLANES), m, v_hi)
        vals.append(m)
    return vals, rank, ew, v_lo, v_hi


def _top16_distinct(s, sub_iota):
    s0 = s
    rank = jnp.full(s.shape, NOT_SELECTED, F32)
    v_lo = jnp.zeros((SUBLANES, LANES), F32)
    v_hi = jnp.zeros((SUBLANES, LANES), F32)
    vals = []
    for it in range(TOPK):
        m = _max0(s)
        hit = s == m
        s = jnp.where(hit, NEG_INF, s)
        rank = jnp.where(hit, float(it), rank)
        if it < SUBLANES:
            v_lo = jnp.where(sub_iota == float(it), m, v_lo)
        else:
            v_hi = jnp.where(sub_iota == float(it - SUBLANES), m, v_hi)
        vals.append(m)
    taken = s == NEG_INF
    ew = jnp.where(taken, jnp.exp(s0 - vals[0]), 0.0)
    n_taken = jnp.sum(jnp.where(taken, 1.0, 0.0), axis=0, keepdims=True)
    ok = jnp.max(jnp.abs(n_taken - float(TOPK))) == 0.0
    return vals, rank, ew, v_lo, v_hi, ok


def _select16(vals1, v2_lo, v2_hi, sub_iota, tie_break=True):
    slabs, pos, n_invalid = [], [], []
    for a in range(TOPK):
        nb = min(TOPK // (a + 1), SUBLANES)
        c = vals1[a] + v2_lo
        if nb < SUBLANES:
            c = jnp.where(sub_iota < float(nb), c, NEG_INF)
        slabs.append(c)
        pos.append(sub_iota + float(TOPK * a))
        n_invalid.append(float(SUBLANES - nb))
    slabs.append(vals1[0] + v2_hi)
    pos.append(sub_iota + float(SUBLANES))
    n_invalid.append(0.0)

    def tree(op, xs):
        xs = list(xs)
        while len(xs) > 1:
            xs = [op(xs[i], xs[i + 1]) if i + 1 < len(xs) else xs[i]
                  for i in range(0, len(xs), 2)]
        return xs[0]

    z = None
    c_max = None
    for it in range(TOPK):
        m = _max0(tree(jnp.maximum, slabs))
        if tie_break:
            first = _min0(tree(jnp.minimum,
                               [jnp.where(c == m, p, 999.0) for c, p in zip(slabs, pos)]))
            slabs = [jnp.where(p == first, NEG_INF, c) for c, p in zip(slabs, pos)]
        else:
            slabs = [jnp.where(c == m, NEG_INF, c) for c in slabs]
        if it == 0:
            c_max = m
            z = jnp.ones_like(m)
        else:
            z = z + jnp.exp(m - c_max)
    counts = []
    for i, c in enumerate(slabs):
        taken = jnp.sum(jnp.where(c == NEG_INF, 1.0, 0.0), axis=0, keepdims=True)
        counts.append(taken - n_invalid[i])
    counts[0] = counts[0] + counts.pop()
    return counts, z


def _route_kernel(hnt_ref, wq_ref, keys_ref, a1_ref, lrow_ref, r2_ref, a2_ref,
                  q_scr, s_scr, *, tb):
    qt = jnp.dot(wq_ref[...], hnt_ref[...], preferred_element_type=F32)
    q_scr[...] = qt.astype(BF16)
    for hc in range(2 * PEER_HEADS):
        rows = slice(hc * N_KEYS, (hc + 1) * N_KEYS)
        s_scr[rows, :] = jnp.dot(keys_ref[hc], q_scr[rows, :], preferred_element_type=F32)

    n_lane_groups = tb // LANES
    key_iota = lax.broadcasted_iota(jnp.int32, (N_KEYS, LANES), 0).astype(F32)
    sub_iota = lax.broadcasted_iota(jnp.int32, (SUBLANES, LANES), 0).astype(F32)

    def body(h, carry):
        rows1 = pl.ds(pl.multiple_of(h * 2 * N_KEYS, N_KEYS), N_KEYS)
        rows2 = pl.ds(pl.multiple_of(h * 2 * N_KEYS + N_KEYS, N_KEYS), N_KEYS)

        def emit(lanes, rank1, e1, rank2, e2, counts, z):
            lrow = jnp.zeros((N_KEYS, LANES), F32)
            for a in range(TOPK):
                lrow = jnp.where(rank1 == float(a), counts[a], lrow)
            a1_ref[pl.ds(h, 1), :, lanes] = e1[None]
            lrow_ref[pl.ds(h, 1), :, lanes] = lrow[None]
            r2_ref[pl.ds(h, 1), :, lanes] = rank2.astype(BF16)[None]
            a2_ref[pl.ds(h, 1), :, lanes] = (e2 / z).astype(BF16)[None]

        redo = []
        for g in range(n_lane_groups):
            lanes = slice(g * LANES, (g + 1) * LANES)
            vals1, rank1, e1, _, _, ok1 = _top16_distinct(s_scr[rows1, lanes], sub_iota)
            _, rank2, e2, v2_lo, v2_hi, ok2 = _top16_distinct(s_scr[rows2, lanes], sub_iota)
            counts, z = _select16(vals1, v2_lo, v2_hi, sub_iota, tie_break=False)
            total = counts[0]
            for cnt in counts[1:]:
                total = total + cnt
            ok3 = jnp.max(jnp.abs(total - float(TOPK))) == 0.0
            emit(lanes, rank1, e1, rank2, e2, counts, z)
            redo.append(jnp.logical_not(ok1 & ok2 & ok3))

        for g in range(n_lane_groups):
            lanes = slice(g * LANES, (g + 1) * LANES)

            @pl.when(redo[g])
            def _():
                vals1, rank1, e1, _, _ = _top16(s_scr[rows1, lanes], key_iota, sub_iota)
                _, rank2, e2, v2_lo, v2_hi = _top16(s_scr[rows2, lanes], key_iota, sub_iota)
                counts, z = _select16(vals1, v2_lo, v2_hi, sub_iota)
                emit(lanes, rank1, e1, rank2, e2, counts, z)

        return carry

    lax.fori_loop(0, PEER_HEADS, body, 0)


def _route(hnt, wq_t, keys, tb=256):
    D, T = hnt.shape
    dq = wq_t.shape[0]
    spec = pl.BlockSpec((PEER_HEADS, N_KEYS, tb), lambda i: (0, 0, i))
    shape = jax.ShapeDtypeStruct((PEER_HEADS, N_KEYS, T), F32)
    shape_b = jax.ShapeDtypeStruct((PEER_HEADS, N_KEYS, T), BF16)
    return pl.pallas_call(
        functools.partial(_route_kernel, tb=tb),
        grid=(T // tb,),
        in_specs=[
            pl.BlockSpec((D, tb), lambda i: (0, i)),
            pl.BlockSpec((dq, D), lambda i: (0, 0)),
            pl.BlockSpec((2 * PEER_HEADS, N_KEYS, N_KEYS), lambda i: (0, 0, 0)),
        ],
        out_specs=[spec, spec, spec, spec],
        out_shape=[shape, shape, shape_b, shape_b],
        scratch_shapes=[pltpu.VMEM((dq, tb), BF16), pltpu.VMEM((dq, tb), F32)],
        compiler_params=_cparams(("arbitrary",)),
        name="peer_route",
    )(hnt, wq_t, keys)


def _peer_kernel(hnt_ref, u_ref, vt_ref, a1_ref, lrow_ref, r2_ref, a2_ref, h_ref,
                 y_ref, acc_scr, g_scr, *, et, n_split):
    j = pl.program_id(1)
    tb = hnt_ref.shape[1]
    part = et // n_split
    pack = 2 * SUBLANES

    @pl.when(j == 0)
    def _():
        acc_scr[...] = jnp.zeros(acc_scr.shape, F32)

    def gate_rows(ht, s):
        for r in range(part // N_KEYS):
            i1 = s * (part // N_KEYS) + r
            lb = [jnp.broadcast_to(lrow_ref[h, i1:i1 + 1, :], (pack, tb)).astype(BF16)
                  for h in range(PEER_HEADS)]
            ab = [jnp.broadcast_to(a1_ref[h, i1:i1 + 1, :], (pack, tb)).astype(BF16)
                  for h in range(PEER_HEADS)]
            for c in range(N_KEYS // pack):
                i2 = slice(c * pack, (c + 1) * pack)
                rows = slice(r * N_KEYS + c * pack, r * N_KEYS + (c + 1) * pack)
                w = None
                for h in range(PEER_HEADS):
                    wh = jnp.where(r2_ref[h, i2, :] < lb[h], ab[h] * a2_ref[h, i2, :],
                                   jnp.zeros((), BF16))
                    w = wh if w is None else w + wh
                x = ht[rows, :]
                gelu = 0.5 * x * (1.0 + lax.erf(x * (2.0 ** -0.5)))
                g_scr[s, rows, :] = w * gelu.astype(BF16)

    hnt = hnt_ref[...]
    hts = [jnp.dot(u_ref[s * part:(s + 1) * part, :], hnt, preferred_element_type=F32)
           for s in range(n_split)]
    for s in range(n_split):
        gate_rows(hts[s], s)
        acc_scr[...] += jnp.dot(vt_ref[0, :, s * part:(s + 1) * part], g_scr[s],
                                preferred_element_type=F32)

    @pl.when(j == pl.num_programs(1) - 1)
    def _():
        y_ref[...] = h_ref[...] + acc_scr[...].T


def _peer(hnt, u, vt, a1, lrow, r2, a2, h, tb=512, et=1024, n_split=4):
    D, T = hnt.shape
    E = u.shape[0]
    rows_per_step = et // N_KEYS
    sel_i1 = pl.BlockSpec((PEER_HEADS, rows_per_step, tb), lambda i, j: (0, j, i))
    sel_i2 = pl.BlockSpec((PEER_HEADS, N_KEYS, tb), lambda i, j: (0, 0, i))
    return pl.pallas_call(
        functools.partial(_peer_kernel, et=et, n_split=n_split),
        grid=(T // tb, E // et),
        in_specs=[
            pl.BlockSpec((D, tb), lambda i, j: (0, i)),
            pl.BlockSpec((et, D), lambda i, j: (j, 0)),
            pl.BlockSpec((1, D, et), lambda i, j: (j, 0, 0)),
            sel_i1, sel_i1, sel_i2, sel_i2,
            pl.BlockSpec((tb, D), lambda i, j: (i, 0)),
        ],
        out_specs=pl.BlockSpec((tb, D), lambda i, j: (i, 0)),
        out_shape=jax.ShapeDtypeStruct((T, D), F32),
        scratch_shapes=[pltpu.VMEM((D, tb), F32),
                        pltpu.VMEM((n_split, et // n_split, tb), BF16)],
        compiler_params=_cparams(("arbitrary", "arbitrary")),
        name="peer_dense",
    )(hnt, u, vt, a1, lrow, r2, a2, h)


def _layer(h2, n_batch, seq, attn_g, w_in, b_forget, q_g, k_g, w_out, ffn_g,
           w_pq, sub_keys, u, v):
    T, D = h2.shape
    n_main = 7 * D_GROUP
    w_in_b = w_in.astype(BF16)
    w_f = w_in[:, n_main:].T.astype(BF16)
    proj, gate, flog_t = _inproj(h2, attn_g.reshape(1, D), w_in_b, w_f,
                                 q_g.reshape(1, HEAD_DIM), k_g.reshape(1, HEAD_DIM))
    c = _forget_cumsum(flog_t, b_forget, seq)
    fox = _fox_attention(proj, gate, c.reshape(n_batch * N_HEADS, seq, 1),
                         c.reshape(n_batch * N_HEADS, 1, seq), n_batch, seq)
    sb = _sb_attention(proj, n_batch, seq)
    h_mid, hnt = _outproj(fox, sb, h2, w_out.astype(BF16), ffn_g.reshape(1, D))
    keys = sub_keys.reshape(2 * PEER_HEADS, N_KEYS, -1).astype(BF16)
    a1, lrow, r2, a2 = _route(hnt, w_pq.T.astype(BF16), keys)
    vt = v.astype(BF16).reshape(-1, PEER_ET, D).transpose(0, 2, 1)
    return _peer(hnt, u.astype(BF16), vt, a1, lrow, r2, a2, h_mid, et=PEER_ET)


def kernel(x, attn_norm_g, w_in, b_forget, q_norm_g, k_norm_g, w_out, ffn_norm_g,
           w_peer_q, peer_sub_keys, peer_u, peer_v):
    n_batch, seq, d_model = x.shape
    h = x.reshape(n_batch * seq, d_model)
    for l in range(attn_norm_g.shape[0]):
        h = _layer(h, n_batch, seq, attn_norm_g[l], w_in[l], b_forget[l], q_norm_g[l],
                   k_norm_g[l], w_out[l], ffn_norm_g[l], w_peer_q[l], peer_sub_keys[l],
                   peer_u[l], peer_v[l])
    return h.reshape(n_batch, seq, d_model)
```

```python
import functools

import jax
import jax.numpy as jnp
from jax import lax
from jax.experimental import pallas as pl
from jax.experimental.pallas import tpu as pltpu

F32 = jnp.float32
BF16 = jnp.bfloat16
EPS = 1e-6
HEAD_DIM = 128
N_HEADS = 8
D_GROUP = N_HEADS * HEAD_DIM
N_KEYS = 128
PEER_HEADS = 8
TOPK = 16
LANES = 128
SUBLANES = 8
NEG_INF = float("-inf")
NOT_SELECTED = 99.0
SB_UNDERFLOW_LOG = -105.0
VMEM_LIMIT = 56 * 1024 * 1024
PEER_ET = 1024


def _cparams(sem):
    return pltpu.CompilerParams(dimension_semantics=sem, vmem_limit_bytes=VMEM_LIMIT)


def _log_sigmoid(z):
    return jnp.minimum(z, 0.0) - jnp.log1p(jnp.exp(-jnp.abs(z)))


def _inproj_kernel(x_ref, g_ref, w_ref, wf_ref, qg_ref, kg_ref,
                   out_ref, gate_ref, flog_ref, xn_scr):
    j = pl.program_id(1)

    @pl.when(j == 0)
    def _():
        x = x_ref[...]
        r = lax.rsqrt(jnp.mean(x * x, axis=-1, keepdims=True) + EPS)
        xn = ((x * r) * g_ref[...]).astype(BF16)
        xn_scr[...] = xn
        flog_ref[...] = lax.dot_general(wf_ref[...], xn, (((1,), (1,)), ((), ())),
                                        preferred_element_type=F32)

    p = jnp.dot(xn_scr[...], w_ref[...], preferred_element_type=F32)

    def head_norm(g):
        for h in range(N_HEADS):
            ph = p[:, h * HEAD_DIM:(h + 1) * HEAD_DIM]
            r = lax.rsqrt(jnp.mean(ph * ph, axis=-1, keepdims=True) + EPS)
            out_ref[0, :, h * HEAD_DIM:(h + 1) * HEAD_DIM] = ((ph * r) * g).astype(BF16)

    @pl.when(j == 0)
    def _():
        head_norm(qg_ref[...])

    @pl.when(j == 1)
    def _():
        head_norm(kg_ref[...])

    @pl.when(j == 3)
    def _():
        gate_ref[...] = jax.nn.sigmoid(p)
        out_ref[0] = p.astype(BF16)

    @pl.when((j == 2) | (j >= 4))
    def _():
        out_ref[0] = p.astype(BF16)


def _inproj(x2, g, w_main, w_f, qg, kg, tm=512):
    T, D = x2.shape
    n_groups = w_main.shape[1] // D_GROUP
    return pl.pallas_call(
        _inproj_kernel,
        grid=(T // tm, n_groups),
        in_specs=[
            pl.BlockSpec((tm, D), lambda i, j: (i, 0)),
            pl.BlockSpec((1, D), lambda i, j: (0, 0)),
            pl.BlockSpec((D, D_GROUP), lambda i, j: (0, j)),
            pl.BlockSpec((N_HEADS, D), lambda i, j: (0, 0)),
            pl.BlockSpec((1, HEAD_DIM), lambda i, j: (0, 0)),
            pl.BlockSpec((1, HEAD_DIM), lambda i, j: (0, 0)),
        ],
        out_specs=[
            pl.BlockSpec((1, tm, D_GROUP), lambda i, j: (j, i, 0)),
            pl.BlockSpec((tm, D_GROUP), lambda i, j: (i, 0)),
            pl.BlockSpec((N_HEADS, tm), lambda i, j: (0, i)),
        ],
        out_shape=[
            jax.ShapeDtypeStruct((n_groups, T, D_GROUP), BF16),
            jax.ShapeDtypeStruct((T, D_GROUP), F32),
            jax.ShapeDtypeStruct((N_HEADS, T), F32),
        ],
        scratch_shapes=[pltpu.VMEM((tm, D), BF16)],
        compiler_params=_cparams(("arbitrary", "arbitrary")),
        name="inproj",
    )(x2, g, w_main, w_f, qg, kg)


def _cumsum_kernel(flog_ref, b_ref, c_ref, *, seq):
    n_batch = flog_ref.shape[1] // seq
    lane = lax.broadcasted_iota(jnp.int32, (N_HEADS, seq), 1)
    for b in range(n_batch):
        x = _log_sigmoid(flog_ref[:, b * seq:(b + 1) * seq] + b_ref[...])
        sh = 1
        while sh < seq:
            x = x + jnp.where(lane >= sh, pltpu.roll(x, sh, axis=1), 0.0)
            sh *= 2
        c_ref[b * N_HEADS:(b + 1) * N_HEADS, :] = x


def _forget_cumsum(flog_t, b_forget, seq):
    T = flog_t.shape[1]
    n_batch = T // seq
    return pl.pallas_call(
        functools.partial(_cumsum_kernel, seq=seq),
        out_shape=jax.ShapeDtypeStruct((n_batch * N_HEADS, seq), F32),
        compiler_params=pltpu.CompilerParams(vmem_limit_bytes=VMEM_LIMIT),
        name="forget_cumsum",
    )(flog_t, b_forget.reshape(N_HEADS, 1))


def _fox_kernel(q_ref, k_ref, v_ref, gate_ref, cq_ref, ck_ref, o_ref,
                m_scr, acc_scr, *, tq, n_par, scale):
    qi = pl.program_id(1)
    m_scr[...] = jnp.full(m_scr.shape, NEG_INF, F32)
    acc_scr[...] = jnp.zeros(acc_scr.shape, F32)
    ones = jnp.ones((tq, HEAD_DIM), BF16)
    heads = [slice(g * HEAD_DIM, (g + 1) * HEAD_DIM) for g in range(n_par)]

    def block(kb, diagonal):
        off = pl.multiple_of(kb * tq, tq)
        for g in range(n_par):
            k = k_ref[0, pl.ds(off, tq), heads[g]]
            v = v_ref[0, pl.ds(off, tq), heads[g]]
            ck = ck_ref[g, :, pl.ds(off, tq)]
            s = lax.dot_general(q_ref[0, :, heads[g]], k, (((1,), (1,)), ((), ())),
                                preferred_element_type=F32)
            s = s * scale + cq_ref[g] - ck
            if diagonal:
                row = lax.broadcasted_iota(jnp.int32, (tq, tq), 0)
                col = lax.broadcasted_iota(jnp.int32, (tq, tq), 1)
                s = jnp.where(col <= row, s, NEG_INF)
            m = m_scr[g]
            m_new = jnp.maximum(m, jnp.max(s, axis=1, keepdims=True))
            alpha = jnp.exp(m - m_new)
            p = jnp.exp(s - m_new)
            pv = jnp.dot(p.astype(BF16), jnp.concatenate([v, ones], axis=1),
                         preferred_element_type=F32)
            acc_scr[g] = acc_scr[g] * alpha + pv
            m_scr[g] = m_new

    def body(kb, carry):
        block(kb, False)
        return carry

    block(qi, True)
    lax.fori_loop(0, qi, body, 0)
    for g in range(n_par):
        acc = acc_scr[g]
        o_ref[:, heads[g]] = ((acc[:, :HEAD_DIM] / acc[:, HEAD_DIM:])
                              * gate_ref[:, heads[g]]).astype(BF16)


def _fox_attention(proj, gate, c_col, c_row, n_batch, seq, tq=512, n_par=4):
    T = proj.shape[1]
    nq = seq // tq
    n_grp = N_HEADS // n_par
    w = n_par * HEAD_DIM
    return pl.pallas_call(
        functools.partial(_fox_kernel, tq=tq, n_par=n_par, scale=HEAD_DIM ** -0.5),
        grid=(n_batch * n_grp, nq),
        in_specs=[
            pl.BlockSpec((1, tq, w), lambda bg, i: (0, (bg // n_grp) * nq + i, bg % n_grp)),
            pl.BlockSpec((1, seq, w), lambda bg, i: (1, bg // n_grp, bg % n_grp)),
            pl.BlockSpec((1, seq, w), lambda bg, i: (2, bg // n_grp, bg % n_grp)),
            pl.BlockSpec((tq, w), lambda bg, i: ((bg // n_grp) * nq + i, bg % n_grp)),
            pl.BlockSpec((n_par, tq, 1), lambda bg, i: (bg, i, 0)),
            pl.BlockSpec((n_par, 1, seq), lambda bg, i: (bg, 0, 0)),
        ],
        out_specs=pl.BlockSpec((tq, w), lambda bg, i: ((bg // n_grp) * nq + i, bg % n_grp)),
        out_shape=jax.ShapeDtypeStruct((T, D_GROUP), BF16),
        scratch_shapes=[pltpu.VMEM((n_par, tq, 1), F32),
                        pltpu.VMEM((n_par, tq, 2 * HEAD_DIM), F32)],
        compiler_params=_cparams(("arbitrary", "arbitrary")),
        name="fox_attention",
    )(proj, proj, proj, gate, c_col, c_row)


def _sb_kernel(q_ref, k_ref, v_ref, uo_ref, o_ref, acc_scr, carry_scr, *, tq, n_par, scale):
    qi = pl.program_id(1)
    heads = [slice(p * HEAD_DIM, (p + 1) * HEAD_DIM) for p in range(n_par)]

    def group(p, g, valid):
        off = pl.multiple_of(g * tq, tq)
        k = k_ref[0, pl.ds(off, tq), heads[p]]
        v = v_ref[0, pl.ds(off, tq), heads[p]]
        z = lax.dot_general(q_ref[0, :, heads[p]], k, (((1,), (1,)), ((), ())),
                            preferred_element_type=F32) * scale
        lp = jnp.log(1.0 + jnp.exp(-jnp.abs(z)))
        t = jnp.minimum(z, 0.0)
        log_beta = t - lp
        log_1mb = (t - z) - lp
        if valid is not None:
            log_1mb = jnp.where(valid, log_1mb, 0.0)
        hi = log_1mb.astype(BF16)
        lo = (log_1mb - hi.astype(F32)).astype(BF16)
        cs = jnp.dot(jnp.concatenate([hi, lo], axis=1), uo_ref[...],
                     preferred_element_type=F32)
        return log_beta, cs[:, :tq], cs[:, tq:], v

    row = lax.broadcasted_iota(jnp.int32, (tq, tq), 0)
    col = lax.broadcasted_iota(jnp.int32, (tq, tq), 1)
    causal = col < row
    has_prev = qi > 0
    for p in range(n_par):
        lb0, suf0, tot0, v0 = group(p, qi, causal)
        lb1, suf1, tot1, v1 = group(p, jnp.maximum(qi - 1, 0), has_prev)
        a0 = jnp.where(causal, jnp.exp(lb0 + suf0), 0.0)
        a1 = jnp.where(has_prev, jnp.exp(lb1 + suf1 + tot0), 0.0)
        acc_scr[p] = (jnp.dot(a0.astype(BF16), v0, preferred_element_type=F32)
                      + jnp.dot(a1.astype(BF16), v1, preferred_element_type=F32))
        carry_scr[p] = tot0 + tot1

    def live():
        return jnp.max(carry_scr[...]) > SB_UNDERFLOW_LOG

    def body(state):
        g, _ = state
        for p in range(n_par):
            lb, suf, tot, v = group(p, g, None)
            carry = carry_scr[p]
            a = jnp.exp(lb + suf + carry)
            acc_scr[p] += jnp.dot(a.astype(BF16), v, preferred_element_type=F32)
            carry_scr[p] = carry + tot
        return g - 1, live()

    lax.while_loop(lambda state: (state[0] >= 0) & state[1], body, (qi - 2, live()))
    for p in range(n_par):
        o_ref[:, heads[p]] = acc_scr[p].astype(BF16)


def _suffix_matrix(n):
    j = jnp.arange(2 * n)[:, None] % n
    s = jnp.arange(n)[None, :]
    tri = (j > s).astype(BF16)
    return jnp.concatenate([tri, jnp.ones((2 * n, n), BF16)], axis=1)


def _sb_attention(proj, n_batch, seq, tq=256, n_par=4):
    T = proj.shape[1]
    nq = seq // tq
    n_grp = N_HEADS // n_par
    w = n_par * HEAD_DIM
    return pl.pallas_call(
        functools.partial(_sb_kernel, tq=tq, n_par=n_par, scale=HEAD_DIM ** -0.5),
        grid=(n_batch * n_grp, nq),
        in_specs=[
            pl.BlockSpec((1, tq, w), lambda bg, i: (4, (bg // n_grp) * nq + i, bg % n_grp)),
            pl.BlockSpec((1, seq, w), lambda bg, i: (5, bg // n_grp, bg % n_grp)),
            pl.BlockSpec((1, seq, w), lambda bg, i: (6, bg // n_grp, bg % n_grp)),
            pl.BlockSpec((2 * tq, 2 * tq), lambda bg, i: (0, 0)),
        ],
        out_specs=pl.BlockSpec((tq, w), lambda bg, i: ((bg // n_grp) * nq + i, bg % n_grp)),
        out_shape=jax.ShapeDtypeStruct((T, D_GROUP), BF16),
        scratch_shapes=[pltpu.VMEM((n_par, tq, HEAD_DIM), F32),
                        pltpu.VMEM((n_par, tq, tq), F32)],
        compiler_params=_cparams(("arbitrary", "arbitrary")),
        name="sb_attention",
    )(proj, proj, proj, _suffix_matrix(tq))


def _outproj_kernel(fox_ref, sb_ref, x_ref, wo1_ref, wo2_ref, g_ref, h_ref, hnt_ref):
    h = (x_ref[...]
         + jnp.dot(fox_ref[...], wo1_ref[...], preferred_element_type=F32)
         + jnp.dot(sb_ref[...], wo2_ref[...], preferred_element_type=F32))
    h_ref[...] = h
    r = lax.rsqrt(jnp.mean(h * h, axis=-1, keepdims=True) + EPS)
    hn = (h * r) * g_ref[...]
    hnt_ref[...] = hn.T.astype(BF16)


def _outproj(fox, sb, x2, wo, g, tm=512):
    T, D = x2.shape
    return pl.pallas_call(
        _outproj_kernel,
        grid=(T // tm,),
        in_specs=[
            pl.BlockSpec((tm, D_GROUP), lambda i: (i, 0)),
            pl.BlockSpec((tm, D_GROUP), lambda i: (i, 0)),
            pl.BlockSpec((tm, D), lambda i: (i, 0)),
            pl.BlockSpec((D_GROUP, D), lambda i: (0, 0)),
            pl.BlockSpec((D_GROUP, D), lambda i: (1, 0)),
            pl.BlockSpec((1, D), lambda i: (0, 0)),
        ],
        out_specs=[
            pl.BlockSpec((tm, D), lambda i: (i, 0)),
            pl.BlockSpec((D, tm), lambda i: (0, i)),
        ],
        out_shape=[jax.ShapeDtypeStruct((T, D), F32), jax.ShapeDtypeStruct((D, T), BF16)],
        compiler_params=_cparams(("arbitrary",)),
        name="outproj",
    )(fox, sb, x2, wo, wo, g)


def _max0(x):
    return jnp.max(x, axis=0, keepdims=True)


def _min0(x):
    return jnp.min(x, axis=0, keepdims=True)


def _top16(s, key_iota, sub_iota):
    rank = jnp.full(s.shape, NOT_SELECTED, F32)
    ew = jnp.zeros(s.shape, F32)
    v_lo = jnp.zeros((SUBLANES, LANES), F32)
    v_hi = jnp.zeros((SUBLANES, LANES), F32)
    vals = []
    for it in range(TOPK):
        m = _max0(s)
        first = _min0(jnp.where(s == m, key_iota, float(N_KEYS)))
        hit = key_iota == first
        s = jnp.where(hit, NEG_INF, s)
        rank = jnp.where(hit, float(it), rank)
        ew = jnp.where(hit, jnp.exp(m - vals[0]) if vals else 1.0, ew)
        if it < SUBLANES:
            v_lo = jnp.where(sub_iota == float(it), m, v_lo)
        else:
            v_hi = jnp.where(sub_iota == float(it - SUBLANES), m, v_hi)
        vals.append(m)
    return vals, rank, ew, v_lo, v_hi


def _top16_distinct(s, sub_iota):
    s0 = s
    rank = jnp.full(s.shape, NOT_SELECTED, F32)
    v_lo = jnp.zeros((SUBLANES, LANES), F32)
    v_hi = jnp.zeros((SUBLANES, LANES), F32)
    vals = []
    for it in range(TOPK):
        m = _max0(s)
        hit = s == m
        s = jnp.where(hit, NEG_INF, s)
        rank = jnp.where(hit, float(it), rank)
        if it < SUBLANES:
            v_lo = jnp.where(sub_iota == float(it), m, v_lo)
        else:
            v_hi = jnp.where(sub_iota == float(it - SUBLANES), m, v_hi)
        vals.append(m)
    taken = s == NEG_INF
    ew = jnp.where(taken, jnp.exp(s0 - vals[0]), 0.0)
    n_taken = jnp.sum(jnp.where(taken, 1.0, 0.0), axis=0, keepdims=True)
    ok = jnp.max(jnp.abs(n_taken - float(TOPK))) == 0.0
    return vals, rank, ew, v_lo, v_hi, ok


def _select16(vals1, v2_lo, v2_hi, sub_iota, tie_break=True):
    slabs, pos, n_invalid = [], [], []
    for a in range(TOPK):
        nb = min(TOPK // (a + 1), SUBLANES)
        c = vals1[a] + v2_lo
        if nb < SUBLANES:
            c = jnp.where(sub_iota < float(nb), c, NEG_INF)
        slabs.append(c)
        pos.append(sub_iota + float(TOPK * a))
        n_invalid.append(float(SUBLANES - nb))
    slabs.append(vals1[0] + v2_hi)
    pos.append(sub_iota + float(SUBLANES))
    n_invalid.append(0.0)

    def tree(op, xs):
        xs = list(xs)
        while len(xs) > 1:
            xs = [op(xs[i], xs[i + 1]) if i + 1 < len(xs) else xs[i]
                  for i in range(0, len(xs), 2)]
        return xs[0]

    z = None
    c_max = None
    for it in range(TOPK):
        m = _max0(tree(jnp.maximum, slabs))
        if tie_break:
            first = _min0(tree(jnp.minimum,
                               [jnp.where(c == m, p, 999.0) for c, p in zip(slabs, pos)]))
            slabs = [jnp.where(p == first, NEG_INF, c) for c, p in zip(slabs, pos)]
        else:
            slabs = [jnp.where(c == m, NEG_INF, c) for c in slabs]
        if it == 0:
            c_max = m
            z = jnp.ones_like(m)
        else:
            z = z + jnp.exp(m - c_max)
    counts = []
    for i, c in enumerate(slabs):
        taken = jnp.sum(jnp.where(c == NEG_INF, 1.0, 0.0), axis=0, keepdims=True)
        counts.append(taken - n_invalid[i])
    counts[0] = counts[0] + counts.pop()
    return counts, z


def _route_kernel(hnt_ref, wq_ref, keys_ref, a1_ref, lrow_ref, r2_ref, a2_ref,
                  q_scr, s_scr, *, tb):
    qt = jnp.dot(wq_ref[...], hnt_ref[...], preferred_element_type=F32)
    q_scr[...] = qt.astype(BF16)
    for hc in range(2 * PEER_HEADS):
        rows = slice(hc * N_KEYS, (hc + 1) * N_KEYS)
        s_scr[rows, :] = jnp.dot(keys_ref[hc], q_scr[rows, :], preferred_element_type=F32)

    n_lane_groups = tb // LANES
    key_iota = lax.broadcasted_iota(jnp.int32, (N_KEYS, LANES), 0).astype(F32)
    sub_iota = lax.broadcasted_iota(jnp.int32, (SUBLANES, LANES), 0).astype(F32)

    def body(h, carry):
        rows1 = pl.ds(pl.multiple_of(h * 2 * N_KEYS, N_KEYS), N_KEYS)
        rows2 = pl.ds(pl.multiple_of(h * 2 * N_KEYS + N_KEYS, N_KEYS), N_KEYS)

        def emit(lanes, rank1, e1, rank2, e2, counts, z):
            lrow = jnp.zeros((N_KEYS, LANES), F32)
            for a in range(TOPK):
                lrow = jnp.where(rank1 == float(a), counts[a], lrow)
            a1_ref[pl.ds(h, 1), :, lanes] = e1[None]
            lrow_ref[pl.ds(h, 1), :, lanes] = lrow[None]
            r2_ref[pl.ds(h, 1), :, lanes] = rank2.astype(BF16)[None]
            a2_ref[pl.ds(h, 1), :, lanes] = (e2 / z).astype(BF16)[None]

        redo = []
        for g in range(n_lane_groups):
            lanes = slice(g * LANES, (g + 1) * LANES)
            vals1, rank1, e1, _, _, ok1 = _top16_distinct(s_scr[rows1, lanes], sub_iota)
            _, rank2, e2, v2_lo, v2_hi, ok2 = _top16_distinct(s_scr[rows2, lanes], sub_iota)
            counts, z = _select16(vals1, v2_lo, v2_hi, sub_iota, tie_break=False)
            total = counts[0]
            for cnt in counts[1:]:
                total = total + cnt
            ok3 = jnp.max(jnp.abs(total - float(TOPK))) == 0.0
            emit(lanes, rank1, e1, rank2, e2, counts, z)
            redo.append(jnp.logical_not(ok1 & ok2 & ok3))

        for g in range(n_lane_groups):
            lanes = slice(g * LANES, (g + 1) * LANES)

            @pl.when(redo[g])
            def _():
                vals1, rank1, e1, _, _ = _top16(s_scr[rows1, lanes], key_iota, sub_iota)
                _, rank2, e2, v2_lo, v2_hi = _top16(s_scr[rows2, lanes], key_iota, sub_iota)
                counts, z = _select16(vals1, v2_lo, v2_hi, sub_iota)
                emit(lanes, rank1, e1, rank2, e2, counts, z)

        return carry

    lax.fori_loop(0, PEER_HEADS, body, 0)


def _route(hnt, wq_t, keys, tb=512):
    D, T = hnt.shape
    dq = wq_t.shape[0]
    spec = pl.BlockSpec((PEER_HEADS, N_KEYS, tb), lambda i: (0, 0, i))
    shape = jax.ShapeDtypeStruct((PEER_HEADS, N_KEYS, T), F32)
    shape_b = jax.ShapeDtypeStruct((PEER_HEADS, N_KEYS, T), BF16)
    return pl.pallas_call(
        functools.partial(_route_kernel, tb=tb),
        grid=(T // tb,),
        in_specs=[
            pl.BlockSpec((D, tb), lambda i: (0, i)),
            pl.BlockSpec((dq, D), lambda i: (0, 0)),
            pl.BlockSpec((2 * PEER_HEADS, N_KEYS, N_KEYS), lambda i: (0, 0, 0)),
        ],
        out_specs=[spec, spec, spec, spec],
        out_shape=[shape, shape, shape_b, shape_b],
        scratch_shapes=[pltpu.VMEM((dq, tb), BF16), pltpu.VMEM((dq, tb), F32)],
        compiler_params=_cparams(("arbitrary",)),
        name="peer_route",
    )(hnt, wq_t, keys)


def _peer_kernel(hnt_ref, u_ref, vt_ref, a1_ref, lrow_ref, r2_ref, a2_ref, h_ref,
                 y_ref, acc_scr, g_scr, *, et, n_split):
    j = pl.program_id(1)
    tb = hnt_ref.shape[1]
    part = et // n_split
    pack = 2 * SUBLANES

    @pl.when(j == 0)
    def _():
        acc_scr[...] = jnp.zeros(acc_scr.shape, F32)

    def gate_rows(ht, s):
        for r in range(part // N_KEYS):
            i1 = s * (part // N_KEYS) + r
            lb = [jnp.broadcast_to(lrow_ref[h, i1:i1 + 1, :], (pack, tb)).astype(BF16)
                  for h in range(PEER_HEADS)]
            ab = [jnp.broadcast_to(a1_ref[h, i1:i1 + 1, :], (pack, tb)).astype(BF16)
                  for h in range(PEER_HEADS)]
            for c in range(N_KEYS // pack):
                i2 = slice(c * pack, (c + 1) * pack)
                rows = slice(r * N_KEYS + c * pack, r * N_KEYS + (c + 1) * pack)
                w = None
                for h in range(PEER_HEADS):
                    wh = jnp.where(r2_ref[h, i2, :] < lb[h], ab[h] * a2_ref[h, i2, :],
                                   jnp.zeros((), BF16))
                    w = wh if w is None else w + wh
                x = ht[rows, :]
                gelu = 0.5 * x * (1.0 + lax.erf(x * (2.0 ** -0.5)))
                g_scr[s, rows, :] = w * gelu.astype(BF16)

    hnt = hnt_ref[...]
    hts = [jnp.dot(u_ref[s * part:(s + 1) * part, :], hnt, preferred_element_type=F32)
           for s in range(n_split)]
    for s in range(n_split):
        gate_rows(hts[s], s)
        acc_scr[...] += jnp.dot(vt_ref[0, :, s * part:(s + 1) * part], g_scr[s],
                                preferred_element_type=F32)

    @pl.when(j == pl.num_programs(1) - 1)
    def _():
        y_ref[...] = h_ref[...] + acc_scr[...].T


def _peer(hnt, u, vt, a1, lrow, r2, a2, h, tb=512, et=1024, n_split=4):
    D, T = hnt.shape
    E = u.shape[0]
    rows_per_step = et // N_KEYS
    sel_i1 = pl.BlockSpec((PEER_HEADS, rows_per_step, tb), lambda i, j: (0, j, i))
    sel_i2 = pl.BlockSpec((PEER_HEADS, N_KEYS, tb), lambda i, j: (0, 0, i))
    return pl.pallas_call(
        functools.partial(_peer_kernel, et=et, n_split=n_split),
        grid=(T // tb, E // et),
        in_specs=[
            pl.BlockSpec((D, tb), lambda i, j: (0, i)),
            pl.BlockSpec((et, D), lambda i, j: (j, 0)),
            pl.BlockSpec((1, D, et), lambda i, j: (j, 0, 0)),
            sel_i1, sel_i1, sel_i2, sel_i2,
            pl.BlockSpec((tb, D), lambda i, j: (i, 0)),
        ],
        out_specs=pl.BlockSpec((tb, D), lambda i, j: (i, 0)),
        out_shape=jax.ShapeDtypeStruct((T, D), F32),
        scratch_shapes=[pltpu.VMEM((D, tb), F32),
                        pltpu.VMEM((n_split, et // n_split, tb), BF16)],
        compiler_params=_cparams(("arbitrary", "arbitrary")),
        name="peer_dense",
    )(hnt, u, vt, a1, lrow, r2, a2, h)


def _layer(h2, n_batch, seq, attn_g, w_in, b_forget, q_g, k_g, w_out, ffn_g,
           w_pq, sub_keys, u, v):
    T, D = h2.shape
    n_main = 7 * D_GROUP
    w_in_b = w_in.astype(BF16)
    w_f = w_in[:, n_main:].T.astype(BF16)
    proj, gate, flog_t = _inproj(h2, attn_g.reshape(1, D), w_in_b, w_f,
                                 q_g.reshape(1, HEAD_DIM), k_g.reshape(1, HEAD_DIM))
    c = _forget_cumsum(flog_t, b_forget, seq)
    fox = _fox_attention(proj, gate, c.reshape(n_batch * N_HEADS, seq, 1),
                         c.reshape(n_batch * N_HEADS, 1, seq), n_batch, seq)
    sb = _sb_attention(proj, n_batch, seq)
    h_mid, hnt = _outproj(fox, sb, h2, w_out.astype(BF16), ffn_g.reshape(1, D))
    keys = sub_keys.reshape(2 * PEER_HEADS, N_KEYS, -1).astype(BF16)
    a1, lrow, r2, a2 = _route(hnt, w_pq.astype(BF16).T, keys)
    vt = v.astype(BF16).reshape(-1, PEER_ET, D).transpose(0, 2, 1)
    return _peer(hnt, u.astype(BF16), vt, a1, lrow, r2, a2, h_mid, et=PEER_ET)


def kernel(x, attn_norm_g, w_in, b_forget, q_norm_g, k_norm_g, w_out, ffn_norm_g,
           w_peer_q, peer_sub_keys, peer_u, peer_v):
    n_batch, seq, d_model = x.shape
    h = x.reshape(n_batch * seq, d_model)
    for l in range(attn_norm_g.shape[0]):
        h = _layer(h, n_batch, seq, attn_norm_g[l], w_in[l], b_forget[l], q_norm_g[l],
                   k_norm_g[l], w_out[l], ffn_norm_g[l], w_peer_q[l], peer_sub_keys[l],
                   peer_u[l], peer_v[l])
    return h.reshape(n_batch, seq, d_model)
```

```python
import functools

import jax
import jax.numpy as jnp
from jax import lax
from jax.experimental import pallas as pl
from jax.experimental.pallas import tpu as pltpu

F32 = jnp.float32
BF16 = jnp.bfloat16
EPS = 1e-6
HEAD_DIM = 128
N_HEADS = 8
D_GROUP = N_HEADS * HEAD_DIM
N_KEYS = 128
PEER_HEADS = 8
TOPK = 16
LANES = 128
SUBLANES = 8
NEG_INF = float("-inf")
NOT_SELECTED = 99.0
SB_UNDERFLOW_LOG = -105.0
VMEM_LIMIT = 56 * 1024 * 1024
PEER_ET = 1024


def _cparams(sem):
    return pltpu.CompilerParams(dimension_semantics=sem, vmem_limit_bytes=VMEM_LIMIT)


def _log_sigmoid(z):
    return jnp.minimum(z, 0.0) - jnp.log1p(jnp.exp(-jnp.abs(z)))


def _inproj_kernel(x_ref, g_ref, w_ref, wf_ref, qg_ref, kg_ref,
                   out_ref, gate_ref, flog_ref, xn_scr):
    j = pl.program_id(1)

    @pl.when(j == 0)
    def _():
        x = x_ref[...]
        r = lax.rsqrt(jnp.mean(x * x, axis=-1, keepdims=True) + EPS)
        xn = ((x * r) * g_ref[...]).astype(BF16)
        xn_scr[...] = xn
        flog_ref[...] = lax.dot_general(wf_ref[...], xn, (((1,), (1,)), ((), ())),
                                        preferred_element_type=F32)

    half = D_GROUP // 2
    xn = xn_scr[...]
    ps = [jnp.dot(xn, w_ref[:, c * half:(c + 1) * half], preferred_element_type=F32)
          for c in range(2)]
    gain = jnp.where(j == 0, qg_ref[...], kg_ref[...])
    normed = j < 2
    for c in range(2):
        for h in range(half // HEAD_DIM):
            ph = ps[c][:, h * HEAD_DIM:(h + 1) * HEAD_DIM]
            r = lax.rsqrt(jnp.mean(ph * ph, axis=-1, keepdims=True) + EPS)
            lo = c * half + h * HEAD_DIM
            out_ref[0, :, lo:lo + HEAD_DIM] = jnp.where(normed, (ph * r) * gain, ph).astype(BF16)

    @pl.when(j == 3)
    def _():
        for c in range(2):
            gate_ref[:, c * half:(c + 1) * half] = jax.nn.sigmoid(ps[c])


def _inproj(x2, g, w_main, w_f, qg, kg, tm=512):
    T, D = x2.shape
    n_groups = w_main.shape[1] // D_GROUP
    return pl.pallas_call(
        _inproj_kernel,
        grid=(T // tm, n_groups),
        in_specs=[
            pl.BlockSpec((tm, D), lambda i, j: (i, 0)),
            pl.BlockSpec((1, D), lambda i, j: (0, 0)),
            pl.BlockSpec((D, D_GROUP), lambda i, j: (0, j)),
            pl.BlockSpec((N_HEADS, D), lambda i, j: (0, 0)),
            pl.BlockSpec((1, HEAD_DIM), lambda i, j: (0, 0)),
            pl.BlockSpec((1, HEAD_DIM), lambda i, j: (0, 0)),
        ],
        out_specs=[
            pl.BlockSpec((1, tm, D_GROUP), lambda i, j: (j, i, 0)),
            pl.BlockSpec((tm, D_GROUP), lambda i, j: (i, 0)),
            pl.BlockSpec((N_HEADS, tm), lambda i, j: (0, i)),
        ],
        out_shape=[
            jax.ShapeDtypeStruct((n_groups, T, D_GROUP), BF16),
            jax.ShapeDtypeStruct((T, D_GROUP), F32),
            jax.ShapeDtypeStruct((N_HEADS, T), F32),
        ],
        scratch_shapes=[pltpu.VMEM((tm, D), BF16)],
        compiler_params=_cparams(("arbitrary", "arbitrary")),
        name="inproj",
    )(x2, g, w_main, w_f, qg, kg)


def _cumsum_kernel(flog_ref, b_ref, c_ref, *, seq):
    n_batch = flog_ref.shape[1] // seq
    lane = lax.broadcasted_iota(jnp.int32, (N_HEADS, seq), 1)
    for b in range(n_batch):
        x = _log_sigmoid(flog_ref[:, b * seq:(b + 1) * seq] + b_ref[...])
        sh = 1
        while sh < seq:
            x = x + jnp.where(lane >= sh, pltpu.roll(x, sh, axis=1), 0.0)
            sh *= 2
        c_ref[b * N_HEADS:(b + 1) * N_HEADS, :] = x


def _forget_cumsum(flog_t, b_forget, seq):
    T = flog_t.shape[1]
    n_batch = T // seq
    return pl.pallas_call(
        functools.partial(_cumsum_kernel, seq=seq),
        out_shape=jax.ShapeDtypeStruct((n_batch * N_HEADS, seq), F32),
        compiler_params=pltpu.CompilerParams(vmem_limit_bytes=VMEM_LIMIT),
        name="forget_cumsum",
    )(flog_t, b_forget.reshape(N_HEADS, 1))


def _fox_kernel(q_ref, k_ref, v_ref, gate_ref, cq_ref, ck_ref, o_ref,
                m_scr, acc_scr, *, tq, n_par, scale):
    qi = pl.program_id(1)
    m_scr[...] = jnp.full(m_scr.shape, NEG_INF, F32)
    acc_scr[...] = jnp.zeros(acc_scr.shape, F32)
    ones = jnp.ones((tq, HEAD_DIM), BF16)
    heads = [slice(g * HEAD_DIM, (g + 1) * HEAD_DIM) for g in range(n_par)]

    def block(kb, diagonal):
        off = pl.multiple_of(kb * tq, tq)
        for g in range(n_par):
            k = k_ref[0, pl.ds(off, tq), heads[g]]
            v = v_ref[0, pl.ds(off, tq), heads[g]]
            ck = ck_ref[g, :, pl.ds(off, tq)]
            s = lax.dot_general(q_ref[0, :, heads[g]], k, (((1,), (1,)), ((), ())),
                                preferred_element_type=F32)
            s = s * scale + cq_ref[g] - ck
            if diagonal:
                row = lax.broadcasted_iota(jnp.int32, (tq, tq), 0)
                col = lax.broadcasted_iota(jnp.int32, (tq, tq), 1)
                s = jnp.where(col <= row, s, NEG_INF)
            m = m_scr[g]
            m_new = jnp.maximum(m, jnp.max(s, axis=1, keepdims=True))
            alpha = jnp.exp(m - m_new)
            p = jnp.exp(s - m_new)
            pv = jnp.dot(p.astype(BF16), jnp.concatenate([v, ones], axis=1),
                         preferred_element_type=F32)
            acc_scr[g] = acc_scr[g] * alpha + pv
            m_scr[g] = m_new

    def body(kb, carry):
        block(kb, False)
        return carry

    block(qi, True)
    lax.fori_loop(0, qi, body, 0)
    for g in range(n_par):
        acc = acc_scr[g]
        o_ref[:, heads[g]] = ((acc[:, :HEAD_DIM] / acc[:, HEAD_DIM:])
                              * gate_ref[:, heads[g]]).astype(BF16)


def _fox_attention(proj, gate, c_col, c_row, n_batch, seq, tq=512, n_par=4):
    T = proj.shape[1]
    nq = seq // tq
    n_grp = N_HEADS // n_par
    w = n_par * HEAD_DIM
    return pl.pallas_call(
        functools.partial(_fox_kernel, tq=tq, n_par=n_par, scale=HEAD_DIM ** -0.5),
        grid=(n_batch * n_grp, nq),
        in_specs=[
            pl.BlockSpec((1, tq, w), lambda bg, i: (0, (bg // n_grp) * nq + i, bg % n_grp)),
            pl.BlockSpec((1, seq, w), lambda bg, i: (1, bg // n_grp, bg % n_grp)),
            pl.BlockSpec((1, seq, w), lambda bg, i: (2, bg // n_grp, bg % n_grp)),
            pl.BlockSpec((tq, w), lambda bg, i: ((bg // n_grp) * nq + i, bg % n_grp)),
            pl.BlockSpec((n_par, tq, 1), lambda bg, i: (bg, i, 0)),
            pl.BlockSpec((n_par, 1, seq), lambda bg, i: (bg, 0, 0)),
        ],
        out_specs=pl.BlockSpec((tq, w), lambda bg, i: ((bg // n_grp) * nq + i, bg % n_grp)),
        out_shape=jax.ShapeDtypeStruct((T, D_GROUP), BF16),
        scratch_shapes=[pltpu.VMEM((n_par, tq, 1), F32),
                        pltpu.VMEM((n_par, tq, 2 * HEAD_DIM), F32)],
        compiler_params=_cparams(("arbitrary", "arbitrary")),
        name="fox_attention",
    )(proj, proj, proj, gate, c_col, c_row)


def _sb_kernel(q_ref, k_ref, v_ref, uo_ref, o_ref, acc_scr, carry_scr, *, tq, n_par, scale):
    qi = pl.program_id(1)
    heads = [slice(p * HEAD_DIM, (p + 1) * HEAD_DIM) for p in range(n_par)]

    def group(p, g, valid):
        off = pl.multiple_of(g * tq, tq)
        k = k_ref[0, pl.ds(off, tq), heads[p]]
        v = v_ref[0, pl.ds(off, tq), heads[p]]
        z = lax.dot_general(q_ref[0, :, heads[p]], k, (((1,), (1,)), ((), ())),
                            preferred_element_type=F32) * scale
        lp = jnp.log(1.0 + jnp.exp(-jnp.abs(z)))
        t = jnp.minimum(z, 0.0)
        log_beta = t - lp
        log_1mb = (t - z) - lp
        if valid is not None:
            log_1mb = jnp.where(valid, log_1mb, 0.0)
        hi = log_1mb.astype(BF16)
        lo = (log_1mb - hi.astype(F32)).astype(BF16)
        cs = jnp.dot(jnp.concatenate([hi, lo], axis=1), uo_ref[...],
                     preferred_element_type=F32)
        return log_beta, cs[:, :tq], cs[:, tq:], v

    row = lax.broadcasted_iota(jnp.int32, (tq, tq), 0)
    col = lax.broadcasted_iota(jnp.int32, (tq, tq), 1)
    causal = col < row
    has_prev = qi > 0
    for p in range(n_par):
        lb0, suf0, tot0, v0 = group(p, qi, causal)
        lb1, suf1, tot1, v1 = group(p, jnp.maximum(qi - 1, 0), has_prev)
        a0 = jnp.where(causal, jnp.exp(lb0 + suf0), 0.0)
        a1 = jnp.where(has_prev, jnp.exp(lb1 + suf1 + tot0), 0.0)
        acc_scr[p] = (jnp.dot(a0.astype(BF16), v0, preferred_element_type=F32)
                      + jnp.dot(a1.astype(BF16), v1, preferred_element_type=F32))
        carry_scr[p] = tot0 + tot1

    def live():
        return jnp.max(carry_scr[...]) > SB_UNDERFLOW_LOG

    def body(state):
        g, _ = state
        for p in range(n_par):
            lb, suf, tot, v = group(p, g, None)
            carry = carry_scr[p]
            a = jnp.exp(lb + suf + carry)
            acc_scr[p] += jnp.dot(a.astype(BF16), v, preferred_element_type=F32)
            carry_scr[p] = carry + tot
        return g - 1, live()

    lax.while_loop(lambda state: (state[0] >= 0) & state[1], body, (qi - 2, live()))
    for p in range(n_par):
        o_ref[:, heads[p]] = acc_scr[p].astype(BF16)


def _suffix_matrix(n):
    j = jnp.arange(2 * n)[:, None] % n
    s = jnp.arange(n)[None, :]
    tri = (j > s).astype(BF16)
    return jnp.concatenate([tri, jnp.ones((2 * n, n), BF16)], axis=1)


def _sb_attention(proj, n_batch, seq, tq=256, n_par=4):
    T = proj.shape[1]
    nq = seq // tq
    n_grp = N_HEADS // n_par
    w = n_par * HEAD_DIM
    return pl.pallas_call(
        functools.partial(_sb_kernel, tq=tq, n_par=n_par, scale=HEAD_DIM ** -0.5),
        grid=(n_batch * n_grp, nq),
        in_specs=[
            pl.BlockSpec((1, tq, w), lambda bg, i: (4, (bg // n_grp) * nq + i, bg % n_grp)),
            pl.BlockSpec((1, seq, w), lambda bg, i: (5, bg // n_grp, bg % n_grp)),
            pl.BlockSpec((1, seq, w), lambda bg, i: (6, bg // n_grp, bg % n_grp)),
            pl.BlockSpec((2 * tq, 2 * tq), lambda bg, i: (0, 0)),
        ],
        out_specs=pl.BlockSpec((tq, w), lambda bg, i: ((bg // n_grp) * nq + i, bg % n_grp)),
        out_shape=jax.ShapeDtypeStruct((T, D_GROUP), BF16),
        scratch_shapes=[pltpu.VMEM((n_par, tq, HEAD_DIM), F32),
                        pltpu.VMEM((n_par, tq, tq), F32)],
        compiler_params=_cparams(("arbitrary", "arbitrary")),
        name="sb_attention",
    )(proj, proj, proj, _suffix_matrix(tq))


def _outproj_kernel(fox_ref, sb_ref, x_ref, wo1_ref, wo2_ref, g_ref, h_ref, hnt_ref):
    h = (x_ref[...]
         + jnp.dot(fox_ref[...], wo1_ref[...], preferred_element_type=F32)
         + jnp.dot(sb_ref[...], wo2_ref[...], preferred_element_type=F32))
    h_ref[...] = h
    r = lax.rsqrt(jnp.mean(h * h, axis=-1, keepdims=True) + EPS)
    hn = (h * r) * g_ref[...]
    hnt_ref[...] = hn.T.astype(BF16)


def _outproj(fox, sb, x2, wo, g, tm=512):
    T, D = x2.shape
    return pl.pallas_call(
        _outproj_kernel,
        grid=(T // tm,),
        in_specs=[
            pl.BlockSpec((tm, D_GROUP), lambda i: (i, 0)),
            pl.BlockSpec((tm, D_GROUP), lambda i: (i, 0)),
            pl.BlockSpec((tm, D), lambda i: (i, 0)),
            pl.BlockSpec((D_GROUP, D), lambda i: (0, 0)),
            pl.BlockSpec((D_GROUP, D), lambda i: (1, 0)),
            pl.BlockSpec((1, D), lambda i: (0, 0)),
        ],
        out_specs=[
            pl.BlockSpec((tm, D), lambda i: (i, 0)),
            pl.BlockSpec((D, tm), lambda i: (0, i)),
        ],
        out_shape=[jax.ShapeDtypeStruct((T, D), F32), jax.ShapeDtypeStruct((D, T), BF16)],
        compiler_params=_cparams(("arbitrary",)),
        name="outproj",
    )(fox, sb, x2, wo, wo, g)


def _max0(x):
    return jnp.max(x, axis=0, keepdims=True)


def _min0(x):
    return jnp.min(x, axis=0, keepdims=True)


def _top16(s, key_iota, sub_iota):
    rank = jnp.full(s.shape, NOT_SELECTED, F32)
    ew = jnp.zeros(s.shape, F32)
    v_lo = jnp.zeros((SUBLANES, LANES), F32)
    v_hi = jnp.zeros((SUBLANES, LANES), F32)
    vals = []
    for it in range(TOPK):
        m = _max0(s)
        first = _min0(jnp.where(s == m, key_iota, float(N_KEYS)))
        hit = key_iota == first
        s = jnp.where(hit, NEG_INF, s)
        rank = jnp.where(hit, float(it), rank)
        ew = jnp.where(hit, jnp.exp(m - vals[0]) if vals else 1.0, ew)
        if it < SUBLANES:
            v_lo = jnp.where(sub_iota == float(it), m, v_lo)
        else:
            v_hi = jnp.where(sub_iota == float(it - SUBLANES), m, v_hi)
        vals.append(m)
    return vals, rank, ew, v_lo, v_hi


def _top16_distinct(s, sub_iota):
    s0 = s
    rank = jnp.full(s.shape, NOT_SELECTED, F32)
    v_lo = jnp.zeros((SUBLANES, LANES), F32)
    v_hi = jnp.zeros((SUBLANES, LANES), F32)
    vals = []
    for it in range(TOPK):
        m = _max0(s)
        hit = s == m
        s = jnp.where(hit, NEG_INF, s)
        rank = jnp.where(hit, float(it), rank)
        if it < SUBLANES:
            v_lo = jnp.where(sub_iota == float(it), m, v_lo)
        else:
            v_hi = jnp.where(sub_iota == float(it - SUBLANES), m, v_hi)
        vals.append(m)
    taken = s == NEG_INF
    ew = jnp.where(taken, jnp.exp(s0 - vals[0]), 0.0)
    n_taken = jnp.sum(jnp.where(taken, 1.0, 0.0), axis=0, keepdims=True)
    ok = jnp.max(jnp.abs(n_taken - float(TOPK))) == 0.0
    return vals, rank, ew, v_lo, v_hi, ok


def _select16(vals1, v2_lo, v2_hi, sub_iota, tie_break=True):
    slabs, pos, n_invalid = [], [], []
    for a in range(TOPK):
        nb = min(TOPK // (a + 1), SUBLANES)
        c = vals1[a] + v2_lo
        if nb < SUBLANES:
            c = jnp.where(sub_iota < float(nb), c, NEG_INF)
        slabs.append(c)
        pos.append(sub_iota + float(TOPK * a))
        n_invalid.append(float(SUBLANES - nb))
    slabs.append(vals1[0] + v2_hi)
    pos.append(sub_iota + float(SUBLANES))
    n_invalid.append(0.0)

    def tree(op, xs):
        xs = list(xs)
        while len(xs) > 1:
            xs = [op(xs[i], xs[i + 1]) if i + 1 < len(xs) else xs[i]
                  for i in range(0, len(xs), 2)]
        return xs[0]

    z = None
    c_max = None
    for it in range(TOPK):
        m = _max0(tree(jnp.maximum, slabs))
        if tie_break:
            first = _min0(tree(jnp.minimum,
                               [jnp.where(c == m, p, 999.0) for c, p in zip(slabs, pos)]))
            slabs = [jnp.where(p == first, NEG_INF, c) for c, p in zip(slabs, pos)]
        else:
            slabs = [jnp.where(c == m, NEG_INF, c) for c in slabs]
        if it == 0:
            c_max = m
            z = jnp.ones_like(m)
        else:
            z = z + jnp.exp(m - c_max)
    counts = []
    for i, c in enumerate(slabs):
        taken = jnp.sum(jnp.where(c == NEG_INF, 1.0, 0.0), axis=0, keepdims=True)
        counts.append(taken - n_invalid[i])
    counts[0] = counts[0] + counts.pop()
    return counts, z


def _route_kernel(hnt_ref, wq_ref, keys_ref, a1_ref, lrow_ref, r2_ref, a2_ref,
                  q_scr, s_scr, *, tb):
    qt = jnp.dot(wq_ref[...], hnt_ref[...], preferred_element_type=F32)
    q_scr[...] = qt.astype(BF16)
    for hc in range(2 * PEER_HEADS):
        rows = slice(hc * N_KEYS, (hc + 1) * N_KEYS)
        s_scr[rows, :] = jnp.dot(keys_ref[hc], q_scr[rows, :], preferred_element_type=F32)

    n_lane_groups = tb // LANES
    key_iota = lax.broadcasted_iota(jnp.int32, (N_KEYS, LANES), 0).astype(F32)
    sub_iota = lax.broadcasted_iota(jnp.int32, (SUBLANES, LANES), 0).astype(F32)

    def body(h, carry):
        rows1 = pl.ds(pl.multiple_of(h * 2 * N_KEYS, N_KEYS), N_KEYS)
        rows2 = pl.ds(pl.multiple_of(h * 2 * N_KEYS + N_KEYS, N_KEYS), N_KEYS)

        def emit(lanes, rank1, e1, rank2, e2, counts, z):
            lrow = jnp.zeros((N_KEYS, LANES), F32)
            for a in range(TOPK):
                lrow = jnp.where(rank1 == float(a), counts[a], lrow)
            a1_ref[pl.ds(h, 1), :, lanes] = e1[None]
            lrow_ref[pl.ds(h, 1), :, lanes] = lrow[None]
            r2_ref[pl.ds(h, 1), :, lanes] = rank2.astype(BF16)[None]
            a2_ref[pl.ds(h, 1), :, lanes] = (e2 / z).astype(BF16)[None]

        redo = []
        for g in range(n_lane_groups):
            lanes = slice(g * LANES, (g + 1) * LANES)
            vals1, rank1, e1, _, _, ok1 = _top16_distinct(s_scr[rows1, lanes], sub_iota)
            _, rank2, e2, v2_lo, v2_hi, ok2 = _top16_distinct(s_scr[rows2, lanes], sub_iota)
            counts, z = _select16(vals1, v2_lo, v2_hi, sub_iota, tie_break=False)
            total = counts[0]
            for cnt in counts[1:]:
                total = total + cnt
            ok3 = jnp.max(jnp.abs(total - float(TOPK))) == 0.0
            emit(lanes, rank1, e1, rank2, e2, counts, z)
            redo.append(jnp.logical_not(ok1 & ok2 & ok3))

        for g in range(n_lane_groups):
            lanes = slice(g * LANES, (g + 1) * LANES)

            @pl.when(redo[g])
            def _():
                vals1, rank1, e1, _, _ = _top16(s_scr[rows1, lanes], key_iota, sub_iota)
                _, rank2, e2, v2_lo, v2_hi = _top16(s_scr[rows2, lanes], key_iota, sub_iota)
                counts, z = _select16(vals1, v2_lo, v2_hi, sub_iota)
                emit(lanes, rank1, e1, rank2, e2, counts, z)

        return carry

    lax.fori_loop(0, PEER_HEADS, body, 0)


def _route(hnt, wq_t, keys, tb=512):
    D, T = hnt.shape
    dq = wq_t.shape[0]
    spec = pl.BlockSpec((PEER_HEADS, N_KEYS, tb), lambda i: (0, 0, i))
    shape = jax.ShapeDtypeStruct((PEER_HEADS, N_KEYS, T), F32)
    shape_b = jax.ShapeDtypeStruct((PEER_HEADS, N_KEYS, T), BF16)
    return pl.pallas_call(
        functools.partial(_route_kernel, tb=tb),
        grid=(T // tb,),
        in_specs=[
            pl.BlockSpec((D, tb), lambda i: (0, i)),
            pl.BlockSpec((dq, D), lambda i: (0, 0)),
            pl.BlockSpec((2 * PEER_HEADS, N_KEYS, N_KEYS), lambda i: (0, 0, 0)),
        ],
        out_specs=[spec, spec, spec, spec],
        out_shape=[shape, shape, shape_b, shape_b],
        scratch_shapes=[pltpu.VMEM((dq, tb), BF16), pltpu.VMEM((dq, tb), F32)],
        compiler_params=_cparams(("arbitrary",)),
        name="peer_route",
    )(hnt, wq_t, keys)


def _peer_kernel(hnt_ref, u_ref, vt_ref, a1_ref, lrow_ref, r2_ref, a2_ref, h_ref,
                 y_ref, acc_scr, g_scr, *, et, n_split):
    j = pl.program_id(1)
    tb = hnt_ref.shape[1]
    part = et // n_split
    pack = 2 * SUBLANES

    @pl.when(j == 0)
    def _():
        acc_scr[...] = jnp.zeros(acc_scr.shape, F32)

    def gate_rows(ht, s):
        for r in range(part // N_KEYS):
            i1 = s * (part // N_KEYS) + r
            lb = [jnp.broadcast_to(lrow_ref[h, i1:i1 + 1, :], (pack, tb)).astype(BF16)
                  for h in range(PEER_HEADS)]
            ab = [jnp.broadcast_to(a1_ref[h, i1:i1 + 1, :], (pack, tb)).astype(BF16)
                  for h in range(PEER_HEADS)]
            for c in range(N_KEYS // pack):
                i2 = slice(c * pack, (c + 1) * pack)
                rows = slice(r * N_KEYS + c * pack, r * N_KEYS + (c + 1) * pack)
                w = None
                for h in range(PEER_HEADS):
                    wh = jnp.where(r2_ref[h, i2, :] < lb[h], ab[h] * a2_ref[h, i2, :],
                                   jnp.zeros((), BF16))
                    w = wh if w is None else w + wh
                x = ht[rows, :]
                gelu = 0.5 * x * (1.0 + lax.erf(x * (2.0 ** -0.5)))
                g_scr[s, rows, :] = w * gelu.astype(BF16)

    hnt = hnt_ref[...]
    hts = [jnp.dot(u_ref[s * part:(s + 1) * part, :], hnt, preferred_element_type=F32)
           for s in range(n_split)]
    for s in range(n_split):
        gate_rows(hts[s], s)
        acc_scr[...] += jnp.dot(vt_ref[0, :, s * part:(s + 1) * part], g_scr[s],
                                preferred_element_type=F32)

    @pl.when(j == pl.num_programs(1) - 1)
    def _():
        y_ref[...] = h_ref[...] + acc_scr[...].T


def _peer(hnt, u, vt, a1, lrow, r2, a2, h, tb=512, et=1024, n_split=4):
    D, T = hnt.shape
    E = u.shape[0]
    rows_per_step = et // N_KEYS
    sel_i1 = pl.BlockSpec((PEER_HEADS, rows_per_step, tb), lambda i, j: (0, j, i))
    sel_i2 = pl.BlockSpec((PEER_HEADS, N_KEYS, tb), lambda i, j: (0, 0, i))
    return pl.pallas_call(
        functools.partial(_peer_kernel, et=et, n_split=n_split),
        grid=(T // tb, E // et),
        in_specs=[
            pl.BlockSpec((D, tb), lambda i, j: (0, i)),
            pl.BlockSpec((et, D), lambda i, j: (j, 0)),
            pl.BlockSpec((1, D, et), lambda i, j: (j, 0, 0)),
            sel_i1, sel_i1, sel_i2, sel_i2,
            pl.BlockSpec((tb, D), lambda i, j: (i, 0)),
        ],
        out_specs=pl.BlockSpec((tb, D), lambda i, j: (i, 0)),
        out_shape=jax.ShapeDtypeStruct((T, D), F32),
        scratch_shapes=[pltpu.VMEM((D, tb), F32),
                        pltpu.VMEM((n_split, et // n_split, tb), BF16)],
        compiler_params=_cparams(("arbitrary", "arbitrary")),
        name="peer_dense",
    )(hnt, u, vt, a1, lrow, r2, a2, h)


def _layer(h2, n_batch, seq, attn_g, w_in, b_forget, q_g, k_g, w_out, ffn_g,
           w_pq, sub_keys, u, v):
    T, D = h2.shape
    n_main = 7 * D_GROUP
    w_in_b = w_in.astype(BF16)
    w_f = w_in[:, n_main:].T.astype(BF16)
    proj, gate, flog_t = _inproj(h2, attn_g.reshape(1, D), w_in_b, w_f,
                                 q_g.reshape(1, HEAD_DIM), k_g.reshape(1, HEAD_DIM))
    c = _forget_cumsum(flog_t, b_forget, seq)
    fox = _fox_attention(proj, gate, c.reshape(n_batch * N_HEADS, seq, 1),
                         c.reshape(n_batch * N_HEADS, 1, seq), n_batch, seq)
    sb = _sb_attention(proj, n_batch, seq)
    h_mid, hnt = _outproj(fox, sb, h2, w_out.astype(BF16), ffn_g.reshape(1, D))
    keys = sub_keys.reshape(2 * PEER_HEADS, N_KEYS, -1).astype(BF16)
    a1, lrow, r2, a2 = _route(hnt, w_pq.astype(BF16).T, keys)
    vt = v.astype(BF16).reshape(-1, PEER_ET, D).transpose(0, 2, 1)
    return _peer(hnt, u.astype(BF16), vt, a1, lrow, r2, a2, h_mid, et=PEER_ET)


def kernel(x, attn_norm_g, w_in, b_forget, q_norm_g, k_norm_g, w_out, ffn_norm_g,
           w_peer_q, peer_sub_keys, peer_u, peer_v):
    n_batch, seq, d_model = x.shape
    h = x.reshape(n_batch * seq, d_model)
    for l in range(attn_norm_g.shape[0]):
        h = _layer(h, n_batch, seq, attn_norm_g[l], w_in[l], b_forget[l], q_norm_g[l],
                   k_norm_g[l], w_out[l], ffn_norm_g[l], w_peer_q[l], peer_sub_keys[l],
                   peer_u[l], peer_v[l])
    return h.reshape(n_batch, seq, d_model)
```
